```python
import jax, jax.numpy as jnp
from jax import lax
import numpy as np

D_MODEL = 1024
BATCH = 4
SEQ = 4096
DEPTH = 4
DEC_BATCH = 128
DEC_SEQ = 1
PAST_LEN = 8192
PAGE_SIZE = 128

N_EVEN_LAYERS = (DEPTH + 1) // 2
N_ODD_LAYERS = DEPTH // 2
Q_BLOCK = 128
RMS_EPS = 1e-6
NEG_INF = -1e30
FOX_HEADS = 8
FOX_HEAD_DIM = 64
FOX_WIDTH = FOX_HEADS * FOX_HEAD_DIM
FOX_SCALE = FOX_HEAD_DIM ** -0.5
GLA_HEADS = 4
GLA_DK = 64
GLA_DV = 128
GLA_KEY_WIDTH = GLA_HEADS * GLA_DK
GLA_WIDTH = GLA_HEADS * GLA_DV
GLA_GATE_RANK = 16
GLA_GATE_NORM = 16.0
GLA_CHUNK = 64
MLA_HEADS = 16
MLA_Q_LORA = 256
MLA_KV_LORA = 128
MLA_NOPE = 64
MLA_ROPE = 32
MLA_QK_DIM = MLA_NOPE + MLA_ROPE
MLA_V = 64
MLA_WIDTH = MLA_HEADS * MLA_V
MLA_SCALE = MLA_QK_DIM ** -0.5
ROPE_THETA = 10000.0
EVEN_SPLITS = (FOX_WIDTH, FOX_WIDTH, FOX_WIDTH, FOX_HEADS, FOX_WIDTH,
               GLA_KEY_WIDTH, GLA_KEY_WIDTH, GLA_WIDTH, GLA_GATE_RANK, GLA_WIDTH)
EVEN_IN_DIM = sum(EVEN_SPLITS)
EVEN_MIX_WIDTH = FOX_WIDTH + GLA_WIDTH
ODD_SPLITS = (MLA_Q_LORA, MLA_KV_LORA, MLA_ROPE, MLA_WIDTH)
ODD_IN_DIM = sum(ODD_SPLITS)

kernel_name = 'hybrid_fox_gla_mla_decode_step'


def _rms(x, gain):
    xf = x.astype(jnp.float32)
    y = xf * lax.rsqrt(jnp.mean(xf * xf, axis=-1, keepdims=True) + RMS_EPS)
    return (y * gain.astype(jnp.float32)).astype(x.dtype)


def _split(z, sizes):
    return jnp.split(z, np.cumsum(sizes)[:-1].tolist(), axis=-1)


def _rope(x, pos):
    half = x.shape[-1] // 2
    inv = ROPE_THETA ** (-jnp.arange(half, dtype=jnp.float32) / half)
    ang = pos.astype(jnp.float32)[:, None] * inv[None, :]
    ang = ang.reshape((1, ang.shape[0]) + (1,) * (x.ndim - 3) + (half,))
    cos, sin = jnp.cos(ang), jnp.sin(ang)
    x1 = x[..., :half].astype(jnp.float32)
    x2 = x[..., half:].astype(jnp.float32)
    return jnp.concatenate([x1 * cos - x2 * sin, x2 * cos + x1 * sin], axis=-1).astype(x.dtype)


def _online_init(b, h, t, dv):
    return (jnp.full((b, h, t), NEG_INF, jnp.float32),
            jnp.zeros((b, h, t), jnp.float32),
            jnp.zeros((b, h, t, dv), jnp.float32))


def _online_update(carry, s, v, spec):
    m, l, acc = carry
    m_new = jnp.maximum(m, s.max(axis=-1))
    alpha = jnp.exp(m - m_new)
    p = jnp.exp(s - m_new[..., None])
    return (m_new, l * alpha + p.sum(axis=-1),
            acc * alpha[..., None] + jnp.einsum(spec, p, v.astype(jnp.float32)))


def _causal_block_attention(q, k, v, scale, log_forget):
    b, s_len, h, _ = q.shape
    dv = v.shape[-1]
    n_blocks = s_len // Q_BLOCK
    kpos = jnp.arange(s_len)
    cum = None
    if log_forget is not None:
        cum = jnp.cumsum(log_forget.astype(jnp.float32), axis=1).transpose(0, 2, 1)

    def block(i):
        start = i * Q_BLOCK
        q_i = lax.dynamic_slice_in_dim(q, start, Q_BLOCK, axis=1)
        s = jnp.einsum('bqhd,bkhd->bhqk', q_i, k).astype(jnp.float32) * scale
        if cum is not None:
            c_i = lax.dynamic_slice_in_dim(cum, start, Q_BLOCK, axis=2)
            s = s + c_i[..., :, None] - cum[..., None, :]
        qpos = start + jnp.arange(Q_BLOCK)
        s = jnp.where(qpos[:, None] >= kpos[None, :], s, NEG_INF)
        p = jax.nn.softmax(s, axis=-1)
        return jnp.einsum('bhqk,bkhd->bqhd', p.astype(v.dtype), v)

    o = lax.map(block, jnp.arange(n_blocks))
    return o.swapaxes(0, 1).reshape(b, s_len, h, dv)


def _fox_paged(q, k_new, v_new, logf_new, cache_k, cache_v, cache_logf, layer, page_table):
    b, t, h, d = q.shape
    n_pages = page_table.shape[1]
    lf_past = cache_logf[layer, page_table].astype(jnp.float32).reshape(b, n_pages * PAGE_SIZE, h)
    suffix = lf_past.sum(axis=1, keepdims=True) - jnp.cumsum(lf_past, axis=1)
    suffix_pages = suffix.reshape(b, n_pages, PAGE_SIZE, h).transpose(1, 0, 3, 2)
    c_new = jnp.cumsum(logf_new.astype(jnp.float32), axis=1).transpose(0, 2, 1)

    def page_step(carry, xs_p):
        phys, sfx = xs_p
        k_p = cache_k[layer, phys]
        v_p = cache_v[layer, phys]
        s = (jnp.einsum('bthd,bshd->bhts', q, k_p).astype(jnp.float32) * FOX_SCALE
             + c_new[..., None] + sfx[:, :, None, :])
        return _online_update(carry, s, v_p, 'bhts,bshd->bhtd'), None

    carry, _ = lax.scan(page_step, _online_init(b, h, t, d), (page_table.T, suffix_pages))
    s = (jnp.einsum('bthd,bshd->bhts', q, k_new).astype(jnp.float32) * FOX_SCALE
         + c_new[..., :, None] - c_new[..., None, :])
    causal = jnp.tril(jnp.ones((t, t), bool))
    s = jnp.where(causal, s, NEG_INF)
    m, l, acc = _online_update(carry, s, v_new, 'bhts,bshd->bhtd')
    return (acc / l[..., None]).transpose(0, 2, 1, 3)


def _gla(q, k, v, log_alpha, state):
    b, t, h, _ = q.shape
    dv = v.shape[-1]
    chunk = GLA_CHUNK if t % GLA_CHUNK == 0 else t
    n = t // chunk

    def to_chunks(a):
        return a.reshape(b, n, chunk, h, a.shape[-1]).transpose(1, 0, 3, 2, 4)

    causal = jnp.tril(jnp.ones((chunk, chunk), bool))

    def step(s, xs_c):
        qc, kc, vc, ac = xs_c
        cum = jnp.cumsum(ac, axis=2)
        o_inter = jnp.einsum('bhcd,bhde->bhce', qc * jnp.exp(cum), s)
        diff = cum[:, :, :, None, :] - cum[:, :, None, :, :]
        decay = jnp.where(causal[:, :, None], jnp.exp(jnp.minimum(diff, 0.0)), 0.0)
        a = jnp.einsum('bhid,bhjd,bhijd->bhij', qc, kc, decay)
        o = o_inter + jnp.einsum('bhij,bhje->bhie', a, vc)
        last = cum[:, :, -1:, :]
        s_new = (s * jnp.exp(last[:, :, 0, :])[..., None]
                 + jnp.einsum('bhcd,bhce->bhde', kc * jnp.exp(last - cum), vc))
        return s_new, o

    s_fin, o = lax.scan(step, state, (to_chunks(q), to_chunks(k), to_chunks(v), to_chunks(log_alpha)))
    return o.transpose(1, 0, 3, 2, 4).reshape(b, t, h, dv), s_fin


def _even_project(h, w_in, b_f, q_gain, k_gain, w_g2, b_g):
    b, t, _ = h.shape
    z = jnp.einsum('btd,de->bte', h, w_in)
    fq, fk, fv, ff, fg, gq, gk, gv, ga, gg = _split(z, EVEN_SPLITS)
    fq = _rms(fq.reshape(b, t, FOX_HEADS, FOX_HEAD_DIM), q_gain)
    fk = _rms(fk.reshape(b, t, FOX_HEADS, FOX_HEAD_DIM), k_gain)
    fv = fv.reshape(b, t, FOX_HEADS, FOX_HEAD_DIM)
    logf = jax.nn.log_sigmoid((ff + b_f).astype(jnp.float32))
    gq = gq.reshape(b, t, GLA_HEADS, GLA_DK) * (GLA_DK ** -0.5)
    gk = gk.reshape(b, t, GLA_HEADS, GLA_DK)
    gv = gv.reshape(b, t, GLA_HEADS, GLA_DV)
    la = jax.nn.log_sigmoid((jnp.einsum('btr,re->bte', ga, w_g2) + b_g).astype(jnp.float32)) / GLA_GATE_NORM
    la = la.reshape(b, t, GLA_HEADS, GLA_DK)
    return fq, fk, fv, logf, fg, gq, gk, gv, la, gg


def _even_output(fox_o, fg, gla_o, gg, gla_gain, w_out):
    b, t = fg.shape[:2]
    fox = fox_o.reshape(b, t, FOX_WIDTH) * jax.nn.silu(fg)
    gla = _rms(gla_o, gla_gain).reshape(b, t, GLA_WIDTH) * jax.nn.silu(gg)
    return jnp.einsum('bte,ed->btd', jnp.concatenate([fox, gla], axis=-1), w_out)


def _mla_project(h, pos, w_in, q_a_gain, w_q_b, kv_a_gain, q_gain):
    b, t, _ = h.shape
    z = jnp.einsum('btd,de->bte', h, w_in)
    q_a, kv_a, k_pe, g = _split(z, ODD_SPLITS)
    q = jnp.einsum('btr,re->bte', _rms(q_a, q_a_gain), w_q_b).reshape(b, t, MLA_HEADS, MLA_QK_DIM)
    q = jnp.concatenate([q[..., :MLA_NOPE], _rope(q[..., MLA_NOPE:], pos)], axis=-1)
    q = _rms(q, q_gain)
    c = _rms(kv_a, kv_a_gain)
    k_pe = _rope(k_pe, pos)
    return q, c, k_pe, g


def _mla_keys(c, k_pe, w_kv_b, k_gain):
    b, s, _ = c.shape
    w = w_kv_b.reshape(MLA_KV_LORA, MLA_HEADS, MLA_NOPE + MLA_V)
    k_nope = jnp.einsum('bsr,rhd->bshd', c, w[..., :MLA_NOPE])
    k = jnp.concatenate([k_nope, jnp.broadcast_to(k_pe[:, :, None, :], (b, s, MLA_HEADS, MLA_ROPE))], axis=-1)
    return _rms(k, k_gain)


def _mla_values(c, w_kv_b):
    w = w_kv_b.reshape(MLA_KV_LORA, MLA_HEADS, MLA_NOPE + MLA_V)
    return jnp.einsum('bsr,rhd->bshd', c, w[..., MLA_NOPE:])


def _mla_paged(q, c_new, kpe_new, cache_ckv, cache_kpe, layer, page_table, w_kv_b, k_gain):
    b, t, h, _ = q.shape

    def page_step(carry, phys):
        c_p = cache_ckv[layer, phys]
        kpe_p = cache_kpe[layer, phys]
        k_p = _mla_keys(c_p, kpe_p, w_kv_b, k_gain)
        s = jnp.einsum('bthd,bshd->bhts', q, k_p).astype(jnp.float32) * MLA_SCALE
        return _online_update(carry, s, c_p, 'bhts,bsr->bhtr'), None

    carry, _ = lax.scan(page_step, _online_init(b, h, t, MLA_KV_LORA), page_table.T)
    k_n = _mla_keys(c_new, kpe_new, w_kv_b, k_gain)
    s = jnp.einsum('bthd,bshd->bhts', q, k_n).astype(jnp.float32) * MLA_SCALE
    s = jnp.where(jnp.tril(jnp.ones((t, t), bool)), s, NEG_INF)
    m, l, acc = _online_update(carry, s, c_new, 'bhts,bsr->bhtr')
    latent = acc / l[..., None]
    w = w_kv_b.reshape(MLA_KV_LORA, MLA_HEADS, MLA_NOPE + MLA_V)
    return jnp.einsum('bhtr,rhd->bthd', latent, w[..., MLA_NOPE:].astype(jnp.float32))


def _mla_output(att, g, w_out):
    b, t = g.shape[:2]
    y = att.reshape(b, t, MLA_WIDTH) * jax.nn.silu(g)
    return jnp.einsum('bte,ed->btd', y, w_out)


def setup_inputs(seed: int = 0) -> dict:
    key = jax.random.key(seed)
    k = jax.random.split(key, 32)
    f32 = jnp.float32
    n_pages = PAST_LEN // PAGE_SIZE
    n_used = DEC_BATCH * n_pages
    n_pool = n_used + n_used // 4

    def normal(kk, shape, scale):
        return jax.random.normal(kk, shape, f32) * scale

    def gain(kk, shape):
        return 1.0 + 0.1 * jax.random.normal(kk, shape, f32)

    page_table = jax.random.permutation(k[9], n_pool)[:n_used].reshape(DEC_BATCH, n_pages).astype(jnp.int32)
    return {
        'x_prompt': jax.random.normal(k[0], (BATCH, SEQ, D_MODEL), f32),
        'x_sample': jax.random.normal(k[1], (DEC_BATCH, DEC_SEQ, D_MODEL), f32),
        'cache_fox_k': jax.random.normal(k[2], (N_EVEN_LAYERS, n_pool, PAGE_SIZE, FOX_HEADS, FOX_HEAD_DIM), f32),
        'cache_fox_v': jax.random.normal(k[3], (N_EVEN_LAYERS, n_pool, PAGE_SIZE, FOX_HEADS, FOX_HEAD_DIM), f32),
        'cache_fox_logf': jax.nn.log_sigmoid(3.5 + 1.5 * jax.random.normal(k[4], (N_EVEN_LAYERS, n_pool, PAGE_SIZE, FOX_HEADS), f32)),
        'cache_mla_ckv': jax.random.normal(k[5], (N_ODD_LAYERS, n_pool, PAGE_SIZE, MLA_KV_LORA), f32),
        'cache_mla_kpe': jax.random.normal(k[6], (N_ODD_LAYERS, n_pool, PAGE_SIZE, MLA_ROPE), f32),
        'state_gla': normal(k[7], (N_EVEN_LAYERS, DEC_BATCH, GLA_HEADS, GLA_DK, GLA_DV), 0.5),
        'page_table': page_table,
        'norm_even': gain(k[10], (N_EVEN_LAYERS, D_MODEL)),
        'w_in_even': normal(k[11], (N_EVEN_LAYERS, D_MODEL, EVEN_IN_DIM), D_MODEL ** -0.5),
        'b_fox_f': jax.random.uniform(k[12], (N_EVEN_LAYERS, FOX_HEADS), f32, 1.0, 6.0),
        'fox_q_gain': gain(k[13], (N_EVEN_LAYERS, FOX_HEAD_DIM)),
        'fox_k_gain': gain(k[14], (N_EVEN_LAYERS, FOX_HEAD_DIM)),
        'gla_w_gate2': normal(k[15], (N_EVEN_LAYERS, GLA_GATE_RANK, GLA_KEY_WIDTH), GLA_GATE_RANK ** -0.5),
        'gla_b_gate': normal(k[16], (N_EVEN_LAYERS, GLA_KEY_WIDTH), 0.1),
        'gla_out_gain': gain(k[17], (N_EVEN_LAYERS, GLA_DV)),
        'w_out_even': normal(k[18], (N_EVEN_LAYERS, EVEN_MIX_WIDTH, D_MODEL), EVEN_MIX_WIDTH ** -0.5),
        'norm_odd': gain(k[19], (N_ODD_LAYERS, D_MODEL)),
        'w_in_odd': normal(k[20], (N_ODD_LAYERS, D_MODEL, ODD_IN_DIM), D_MODEL ** -0.5),
        'mla_q_a_gain': gain(k[21], (N_ODD_LAYERS, MLA_Q_LORA)),
        'w_q_b': normal(k[22], (N_ODD_LAYERS, MLA_Q_LORA, MLA_HEADS * MLA_QK_DIM), MLA_Q_LORA ** -0.5),
        'mla_kv_a_gain': gain(k[23], (N_ODD_LAYERS, MLA_KV_LORA)),
        'w_kv_b': normal(k[24], (N_ODD_LAYERS, MLA_KV_LORA, MLA_HEADS * (MLA_NOPE + MLA_V)), MLA_KV_LORA ** -0.5),
        'mla_q_gain': gain(k[25], (N_ODD_LAYERS, MLA_QK_DIM)),
        'mla_k_gain': gain(k[26], (N_ODD_LAYERS, MLA_QK_DIM)),
        'w_out_odd': normal(k[27], (N_ODD_LAYERS, MLA_WIDTH, D_MODEL), MLA_WIDTH ** -0.5),
    }


def reference(x_prompt, x_sample, cache_fox_k, cache_fox_v, cache_fox_logf, cache_mla_ckv, cache_mla_kpe,
              state_gla, page_table, norm_even, w_in_even, b_fox_f, fox_q_gain, fox_k_gain, gla_w_gate2,
              gla_b_gate, gla_out_gain, w_out_even, norm_odd, w_in_odd, mla_q_a_gain, w_q_b, mla_kv_a_gain,
              w_kv_b, mla_q_gain, mla_k_gain, w_out_odd):
    f32 = jnp.float32
    b_p, t_p, _ = x_prompt.shape
    t_s = x_sample.shape[1]
    past_len = page_table.shape[1] * PAGE_SIZE
    pos_p = jnp.arange(t_p)
    pos_s = past_len + jnp.arange(t_s)
    xp, xs = x_prompt, x_sample
    fk_p, fv_p, lf_p, ckv_p, kpe_p, gla_p = [], [], [], [], [], []
    fk_s, fv_s, lf_s, ckv_s, kpe_s, gla_s = [], [], [], [], [], []
    for layer in range(DEPTH):
        if layer % 2 == 0:
            e = layer // 2
            w_e = (w_in_even[e], b_fox_f[e], fox_q_gain[e], fox_k_gain[e], gla_w_gate2[e], gla_b_gate[e])
            fq, fk, fv, lf, fg, gq, gk, gv, la, gg = _even_project(_rms(xp, norm_even[e]), *w_e)
            fox_o = _causal_block_attention(fq, fk, fv, FOX_SCALE, lf)
            gla_o, s_fin = _gla(gq, gk, gv, la, jnp.zeros((b_p, GLA_HEADS, GLA_DK, GLA_DV), f32))
            xp = xp + _even_output(fox_o, fg, gla_o, gg, gla_out_gain[e], w_out_even[e]).astype(xp.dtype)
            fk_p.append(fk); fv_p.append(fv); lf_p.append(lf); gla_p.append(s_fin)
            fq, fk, fv, lf, fg, gq, gk, gv, la, gg = _even_project(_rms(xs, norm_even[e]), *w_e)
            fox_o = _fox_paged(fq, fk, fv, lf, cache_fox_k, cache_fox_v, cache_fox_logf, e, page_table)
            gla_o, s_fin = _gla(gq, gk, gv, la, state_gla[e].astype(f32))
            xs = xs + _even_output(fox_o, fg, gla_o, gg, gla_out_gain[e], w_out_even[e]).astype(xs.dtype)
            fk_s.append(fk); fv_s.append(fv); lf_s.append(lf); gla_s.append(s_fin)
        else:
            o = layer // 2
            w_o = (w_in_odd[o], mla_q_a_gain[o], w_q_b[o], mla_kv_a_gain[o], mla_q_gain[o])
            q, c, kpe, g = _mla_project(_rms(xp, norm_odd[o]), pos_p, *w_o)
            att = _causal_block_attention(q, _mla_keys(c, kpe, w_kv_b[o], mla_k_gain[o]),
                                          _mla_values(c, w_kv_b[o]), MLA_SCALE, None)
            xp = xp + _mla_output(att, g, w_out_odd[o]).astype(xp.dtype)
            ckv_p.append(c); kpe_p.append(kpe)
            q, c, kpe, g = _mla_project(_rms(xs, norm_odd[o]), pos_s, *w_o)
            att = _mla_paged(q, c, kpe, cache_mla_ckv, cache_mla_kpe, o, page_table, w_kv_b[o], mla_k_gain[o])
            xs = xs + _mla_output(att, g, w_out_odd[o]).astype(xs.dtype)
            ckv_s.append(c); kpe_s.append(kpe)
    return (xp, xs,
            jnp.stack(fk_p).astype(cache_fox_k.dtype), jnp.stack(fv_p).astype(cache_fox_v.dtype),
            jnp.stack(lf_p).astype(cache_fox_logf.dtype),
            jnp.stack(ckv_p).astype(cache_mla_ckv.dtype), jnp.stack(kpe_p).astype(cache_mla_kpe.dtype),
            jnp.stack(gla_p).astype(state_gla.dtype),
            jnp.stack(fk_s).astype(cache_fox_k.dtype), jnp.stack(fv_s).astype(cache_fox_v.dtype),
            jnp.stack(lf_s).astype(cache_fox_logf.dtype),
            jnp.stack(ckv_s).astype(cache_mla_ckv.dtype), jnp.stack(kpe_s).astype(cache_mla_kpe.dtype),
            jnp.stack(gla_s).astype(state_gla.dtype))
```

```python
import functools

import numpy as np
import jax
import jax.numpy as jnp
from jax import lax
from jax.experimental import pallas as pl
from jax.experimental.pallas import tpu as pltpu

F32 = jnp.float32
BF16 = jnp.bfloat16

D_MODEL = 1024
PAGE = 128
RMS_EPS = 1e-6
NEG_INF = -1e30
FOX_H = 8
FOX_D = 64
FOX_W = FOX_H * FOX_D
FOX_SCALE = FOX_D ** -0.5
GLA_H = 4
GLA_DK = 64
GLA_DV = 128
GLA_KW = GLA_H * GLA_DK
GLA_W = GLA_H * GLA_DV
GLA_RANK = 16
GLA_GATE_NORM = 16.0
GLA_CHUNK = 64
MLA_H = 16
MLA_Q_LORA = 256
MLA_KV_LORA = 128
MLA_NOPE = 64
MLA_ROPE = 32
MLA_QK = MLA_NOPE + MLA_ROPE
MLA_V = 64
MLA_W = MLA_H * MLA_V
MLA_SCALE = MLA_QK ** -0.5
MLA_HP = 128
ROPE_THETA = 10000.0

VMEM_LIMIT = 56 * 1024 * 1024

E_Q, E_K, E_V, E_REST, E_MISC, E_END = 0, 512, 1024, 1536, 3584, 3712


def _cparams(sem):
    return pltpu.CompilerParams(dimension_semantics=sem, vmem_limit_bytes=VMEM_LIMIT)


def _log_sigmoid(x):
    return jnp.minimum(x, 0.0) - jnp.log1p(jnp.exp(-jnp.abs(x)))


def _silu(x):
    return x / (1.0 + jnp.exp(-x))


def _split_bf16(x):
    hi = x.astype(BF16)
    lo = (x - hi.astype(F32)).astype(BF16)
    return hi, lo


def _dot(a, b):
    return jnp.dot(a, b, preferred_element_type=F32)


def _dot_nt(a, b):
    return lax.dot_general(a, b, (((1,), (1,)), ((), ())), preferred_element_type=F32)


def _dot_tn(a, b):
    return lax.dot_general(a, b, (((0,), (0,)), ((), ())), preferred_element_type=F32)


def _nmm_kernel(x_ref, g_ref, w_ref, o_ref, *, norm):
    x = x_ref[...]
    if norm:
        ms = jnp.mean(x * x, axis=-1, keepdims=True)
        x = x * lax.rsqrt(ms + RMS_EPS) * g_ref[...]
    o_ref[...] = _dot(x.astype(BF16), w_ref[...])


def _norm_matmul(x, gain, w_bf16, *, norm=True, tm=256):
    m, k = x.shape
    n = w_bf16.shape[1]
    tm = min(tm, m)
    return pl.pallas_call(
        functools.partial(_nmm_kernel, norm=norm),
        grid=(m // tm,),
        in_specs=[pl.BlockSpec((tm, k), lambda i: (i, 0)),
                  pl.BlockSpec((1, k), lambda i: (0, 0)),
                  pl.BlockSpec((k, n), lambda i: (0, 0))],
        out_specs=pl.BlockSpec((tm, n), lambda i: (i, 0)),
        out_shape=jax.ShapeDtypeStruct((m, n), F32),
        compiler_params=_cparams(("parallel",)),
        name="norm_matmul",
    )(x, gain.reshape(1, k).astype(F32), w_bf16)


def _even_in_kernel(x_ref, g_ref, w_ref, bias_ref, qg_ref, kg_ref, ones_ref, tri_ref,
                    qn_ref, kn_ref, knb_ref, v_ref, vb_ref, rest_ref, misc_ref, cum_ref,
                    carry_ref, *, tiles_per_seq, do_cum):
    x = x_ref[...]
    ms = jnp.mean(x * x, axis=-1, keepdims=True)
    xn = (x * lax.rsqrt(ms + RMS_EPS) * g_ref[...]).astype(BF16)

    def mm(lo, hi):
        return _dot(xn, w_ref[:, lo:hi])

    def headnorm(z, gain):
        ss = _dot((z * z).astype(BF16), ones_ref[...])
        return z * lax.rsqrt(ss * (1.0 / FOX_D) + RMS_EPS) * gain

    q = headnorm(mm(E_Q, E_K), qg_ref[...]) * FOX_SCALE
    qn_ref[...] = q.astype(BF16)
    k = headnorm(mm(E_K, E_V), kg_ref[...])
    kn_ref[...] = k
    knb_ref[...] = k.astype(BF16)
    v = mm(E_V, E_REST)
    v_ref[...] = v
    vb_ref[...] = v.astype(BF16)
    rest_ref[...] = mm(E_REST, E_MISC)
    mz = mm(E_MISC, E_END) + bias_ref[...]
    lane = lax.broadcasted_iota(jnp.int32, mz.shape, 1)
    logf = jnp.where(lane < FOX_H, _log_sigmoid(mz), mz)
    misc_ref[...] = logf
    if do_cum:
        @pl.when(pl.program_id(0) % tiles_per_seq == 0)
        def _():
            carry_ref[...] = jnp.zeros_like(carry_ref)

        lf = jnp.where(lane < FOX_H, logf, 0.0)
        hi, lo = _split_bf16(lf)
        c = _dot(tri_ref[...], hi) + _dot(tri_ref[...], lo) + carry_ref[...]
        cum_ref[...] = c
        carry_ref[...] = c[c.shape[0] - 1:, :]
    else:
        cum_ref[...] = logf


def _even_in(x, gain, w_bf16, bias128, qg, kg, seq_len, *, tm=256):
    m, k = x.shape
    tm = min(tm, m, seq_len) if seq_len > 1 else min(tm, m)
    do_cum = seq_len > 1
    tiles_per_seq = max(seq_len // tm, 1)
    ones_bd = jnp.asarray(np.kron(np.eye(FOX_H), np.ones((FOX_D, FOX_D))), BF16)
    tri = jnp.asarray(np.tril(np.ones((tm, tm))), BF16)
    row = lambda w: pl.BlockSpec((tm, w), lambda i: (i, 0))
    full = lambda a, b: pl.BlockSpec((a, b), lambda i: (0, 0))
    outs = pl.pallas_call(
        functools.partial(_even_in_kernel, tiles_per_seq=tiles_per_seq, do_cum=do_cum),
        grid=(m // tm,),
        in_specs=[row(k), full(1, k), full(k, E_END), full(1, 128), full(1, FOX_W), full(1, FOX_W),
                  full(FOX_W, FOX_W), full(tm, tm)],
        out_specs=[row(FOX_W), row(FOX_W), row(FOX_W), row(FOX_W), row(FOX_W),
                   row(E_MISC - E_REST), row(128), row(128)],
        out_shape=[jax.ShapeDtypeStruct((m, FOX_W), BF16),
                   jax.ShapeDtypeStruct((m, FOX_W), F32),
                   jax.ShapeDtypeStruct((m, FOX_W), BF16),
                   jax.ShapeDtypeStruct((m, FOX_W), F32),
                   jax.ShapeDtypeStruct((m, FOX_W), BF16),
                   jax.ShapeDtypeStruct((m, E_MISC - E_REST), F32),
                   jax.ShapeDtypeStruct((m, 128), F32),
                   jax.ShapeDtypeStruct((m, 128), F32)],
        scratch_shapes=[pltpu.VMEM((1, 128), F32)],
        compiler_params=_cparams(("arbitrary",)),
        name="even_in_proj",
    )(x, gain.reshape(1, k), w_bf16, bias128, qg, kg, ones_bd, tri)
    return outs


def _flash_kernel(*refs, n_heads, dq, dv, bias):
    if bias:
        q_ref, k_ref, v_ref, g_ref, cq_ref, ck_ref, o_ref, m_ref, l_ref, acc_ref = refs
    else:
        q_ref, k_ref, v_ref, g_ref, o_ref, m_ref, l_ref, acc_ref = refs
    i = pl.program_id(1)
    j = pl.program_id(2)

    @pl.when(j == 0)
    def _():
        m_ref[...] = jnp.full_like(m_ref, NEG_INF)
        l_ref[...] = jnp.zeros_like(l_ref)
        acc_ref[...] = jnp.zeros_like(acc_ref)

    def step(masked):
        t = q_ref.shape[1]
        if masked:
            r = lax.broadcasted_iota(jnp.int32, (t, t), 0)
            c = lax.broadcasted_iota(jnp.int32, (t, t), 1)
            keep = r >= c
        for h in range(n_heads):
            q = q_ref[0, :, h * dq:(h + 1) * dq]
            k = k_ref[0, :, h * dq:(h + 1) * dq]
            s = _dot_nt(q, k)
            if bias:
                s = s + cq_ref[0, :, h:h + 1] - ck_ref[0, h:h + 1, :]
            if masked:
                s = jnp.where(keep, s, NEG_INF)
            m_prev = m_ref[h]
            m_new = jnp.maximum(m_prev, jnp.max(s, axis=-1, keepdims=True))
            alpha = jnp.exp(m_prev - m_new)
            p = jnp.exp(s - m_new)
            l_ref[h] = alpha * l_ref[h] + jnp.sum(p, axis=-1, keepdims=True)
            pv = _dot(p.astype(BF16), v_ref[0, :, h * dv:(h + 1) * dv])
            acc_ref[:, h * dv:(h + 1) * dv] = alpha * acc_ref[:, h * dv:(h + 1) * dv] + pv
            m_ref[h] = m_new

    @pl.when(j < i)
    def _():
        step(False)

    @pl.when(j == i)
    def _():
        step(True)
        for h in range(n_heads):
            sl = slice(h * dv, (h + 1) * dv)
            o = acc_ref[:, sl] / l_ref[h]
            o_ref[0, :, sl] = (o * _silu(g_ref[0, :, sl])).astype(o_ref.dtype)


def _flash(q, k, v, gate, gate_block, n_heads, dq, dv, cq=None, ck=None, *, t_blk=512):
    b, t, _ = q.shape
    tb = min(t_blk, t)
    n = t // tb
    bias = cq is not None
    qmap = lambda bi, i, j: (bi, i, 0)
    kmap = lambda bi, i, j: (bi, jnp.minimum(i, j), 0)
    in_specs = [pl.BlockSpec((1, tb, n_heads * dq), qmap),
                pl.BlockSpec((1, tb, n_heads * dq), kmap),
                pl.BlockSpec((1, tb, n_heads * dv), kmap),
                pl.BlockSpec((1, tb, n_heads * dv), lambda bi, i, j: (bi, i, gate_block))]
    args = [q, k, v, gate]
    if bias:
        in_specs += [pl.BlockSpec((1, tb, 128), qmap),
                     pl.BlockSpec((1, n_heads, tb), lambda bi, i, j: (bi, 0, jnp.minimum(i, j)))]
        args += [cq, ck]
    return pl.pallas_call(
        functools.partial(_flash_kernel, n_heads=n_heads, dq=dq, dv=dv, bias=bias),
        grid=(b, n, n),
        in_specs=in_specs,
        out_specs=pl.BlockSpec((1, tb, n_heads * dv), qmap),
        out_shape=jax.ShapeDtypeStruct((b, t, n_heads * dv), BF16),
        scratch_shapes=[pltpu.VMEM((n_heads, tb, 1), F32),
                        pltpu.VMEM((n_heads, tb, 1), F32),
                        pltpu.VMEM((tb, n_heads * dv), F32)],
        compiler_params=_cparams(("parallel", "parallel", "arbitrary")),
        name="flash_fox" if bias else "flash_mla",
    )(*args)


def _gla_gate(misc, w2_ref, bg_ref):
    pre = _dot(misc.astype(BF16), w2_ref[...]) + bg_ref[...]
    return _log_sigmoid(pre) * (1.0 / GLA_GATE_NORM)


def _gla_kernel(q_ref, k_ref, v_ref, gg_ref, misc_ref, w2_ref, bg_ref, gain_ref, hsel_ref, tri_ref,
                o_ref, sfin_ref, st_ref, cum_s, k_s, w_s):
    c = pl.program_id(1)
    n_c = pl.num_programs(1)
    ch = GLA_CHUNK

    @pl.when(c == 0)
    def _():
        st_ref[...] = jnp.zeros_like(st_ref)

    q = q_ref[0] * (GLA_DK ** -0.5)
    k = k_ref[0]
    v = v_ref[0]
    la = _gla_gate(misc_ref[0], w2_ref, bg_ref)
    hi, lo = _split_bf16(la)
    cum = _dot(tri_ref[...], hi) + _dot(tri_ref[...], lo)
    cum_s[...] = cum
    k_s[...] = k

    def body(j, carry):
        rowc = cum_s[pl.ds(j, 1), :]
        rowk = k_s[pl.ds(j, 1), :]
        w = (jnp.exp(jnp.minimum(cum - rowc, 0.0)) * (q * rowk)).astype(BF16)
        off = pl.multiple_of(j * 128, 128)
        w_s[0, :, pl.ds(off, 128)] = w[:, 0:128]
        w_s[1, :, pl.ds(off, 128)] = w[:, 128:256]
        return carry

    lax.fori_loop(0, ch, body, 0)
    r = lax.broadcasted_iota(jnp.int32, (ch, ch), 0)
    cc = lax.broadcasted_iota(jnp.int32, (ch, ch), 1)
    causal = r >= cc
    for lt in range(2):
        a2 = _dot(w_s[lt], hsel_ref[...])
        for hh in range(2):
            h = 2 * lt + hh
            sk = slice(h * GLA_DK, (h + 1) * GLA_DK)
            sv = slice(h * GLA_DV, (h + 1) * GLA_DV)
            a = jnp.where(causal, a2[:, hh * ch:(hh + 1) * ch], 0.0)
            vh = v[:, sv]
            vb = vh.astype(BF16)
            cum_h = cum[:, sk]
            last = cum_h[ch - 1:, :]
            st = st_ref[h]
            qe = (q[:, sk] * jnp.exp(cum_h)).astype(BF16)
            o = _dot_nt(qe, st.astype(BF16)) + _dot(a.astype(BF16), vb)
            kd = (k[:, sk] * jnp.exp(last - cum_h)).astype(BF16)
            st_ref[h] = st * jnp.exp(last) + _dot(vh.T.astype(BF16), kd)
            ms = jnp.mean(o * o, axis=-1, keepdims=True)
            y = o * lax.rsqrt(ms + RMS_EPS) * gain_ref[...] * _silu(gg_ref[0, :, sv])
            o_ref[0, :, sv] = y.astype(o_ref.dtype)

    @pl.when(c == n_c - 1)
    def _():
        sfin_ref[0] = st_ref[...]


def _gla_prompt(rest, misc, w2pad, bg, gain):
    b, t, _ = rest.shape
    ch = GLA_CHUNK
    n = t // ch
    hsel = np.zeros((ch, 2, GLA_DK, 2, ch), np.float32)
    for hh in range(2):
        for j in range(ch):
            hsel[j, hh, :, hh, j] = 1.0
    hsel = jnp.asarray(hsel.reshape(ch * 128, 128), BF16)
    tri = jnp.asarray(np.tril(np.ones((ch, ch))), BF16)
    full = lambda *s: pl.BlockSpec(s, lambda bi, ci: (0,) * len(s))
    o, sfin = pl.pallas_call(
        _gla_kernel,
        grid=(b, n),
        in_specs=[pl.BlockSpec((1, ch, GLA_KW), lambda bi, ci: (bi, ci, 2)),
                  pl.BlockSpec((1, ch, GLA_KW), lambda bi, ci: (bi, ci, 3)),
                  pl.BlockSpec((1, ch, GLA_W), lambda bi, ci: (bi, ci, 2)),
                  pl.BlockSpec((1, ch, GLA_W), lambda bi, ci: (bi, ci, 3)),
                  pl.BlockSpec((1, ch, 128), lambda bi, ci: (bi, ci, 0)),
                  full(128, GLA_KW), full(1, GLA_KW), full(1, GLA_DV), full(ch * 128, 128), full(ch, ch)],
        out_specs=[pl.BlockSpec((1, ch, GLA_W), lambda bi, ci: (bi, ci, 0)),
                   pl.BlockSpec((1, GLA_H, GLA_DV, GLA_DK), lambda bi, ci: (bi, 0, 0, 0))],
        out_shape=[jax.ShapeDtypeStruct((b, t, GLA_W), BF16),
                   jax.ShapeDtypeStruct((b, GLA_H, GLA_DV, GLA_DK), F32)],
        scratch_shapes=[pltpu.VMEM((GLA_H, GLA_DV, GLA_DK), F32),
                        pltpu.VMEM((ch, GLA_KW), F32),
                        pltpu.VMEM((ch, GLA_KW), F32),
                        pltpu.VMEM((2, ch, ch * 128), BF16)],
        compiler_params=_cparams(("parallel", "arbitrary")),
        name="gla_prompt",
    )(rest, rest, rest, rest, misc, w2pad, bg, gain, hsel, tri)
    return o, sfin


def _gla_step_kernel(qT_ref, kT_ref, gaT_ref, v_ref, gg_ref, s_ref, w2T_ref, bgT_ref, gain_ref,
                     o_ref, snew_ref, *, tb):
    laT = _log_sigmoid(_dot(w2T_ref[...], gaT_ref[0].astype(BF16)) + bgT_ref[...]) * (1.0 / GLA_GATE_NORM)
    eT = jnp.exp(laT)
    for bb in range(tb):
        for h in range(GLA_H):
            sk = slice(h * GLA_DK, (h + 1) * GLA_DK)
            sv = slice(h * GLA_DV, (h + 1) * GLA_DV)
            s_new = s_ref[bb, h] * eT[sk, bb:bb + 1] + kT_ref[0, sk, bb:bb + 1] * v_ref[bb, :, sv]
            snew_ref[bb, h] = s_new
            qcol = qT_ref[0, sk, bb:bb + 1] * (GLA_DK ** -0.5)
            o = jnp.sum(qcol * s_new, axis=0, keepdims=True)
            ms = jnp.mean(o * o, axis=-1, keepdims=True)
            y = o * lax.rsqrt(ms + RMS_EPS) * gain_ref[...] * _silu(gg_ref[bb, :, sv])
            o_ref[bb, :, sv] = y


def _gla_step(rest, misc, state, w2, bg, gain, *, tb=8):
    bd = rest.shape[0]
    tb = min(tb, bd)
    g = bd // tb
    tr = lambda a: a.reshape(g, tb, a.shape[-1]).transpose(0, 2, 1)
    qT = tr(rest[:, 512:768])
    kT = tr(rest[:, 768:1024])
    gaT = tr(misc[:, 8:8 + GLA_RANK])
    v3 = rest[:, 1024:1536].reshape(bd, 1, GLA_W)
    gg3 = rest[:, 1536:2048].reshape(bd, 1, GLA_W)
    blk = lambda *s: pl.BlockSpec(s, lambda i: (i,) + (0,) * (len(s) - 1))
    full = lambda *s: pl.BlockSpec(s, lambda i: (0,) * len(s))
    o, snew = pl.pallas_call(
        functools.partial(_gla_step_kernel, tb=tb),
        grid=(g,),
        in_specs=[blk(1, GLA_KW, tb), blk(1, GLA_KW, tb), blk(1, GLA_RANK, tb),
                  blk(tb, 1, GLA_W), blk(tb, 1, GLA_W), blk(tb, GLA_H, GLA_DK, GLA_DV),
                  full(GLA_KW, GLA_RANK), full(GLA_KW, 1), full(1, GLA_DV)],
        out_specs=[blk(tb, 1, GLA_W), blk(tb, GLA_H, GLA_DK, GLA_DV)],
        out_shape=[jax.ShapeDtypeStruct((bd, 1, GLA_W), F32),
                   jax.ShapeDtypeStruct((bd, GLA_H, GLA_DK, GLA_DV), F32)],
        compiler_params=_cparams(("parallel",)),
        name="gla_step",
    )(qT, kT, gaT, v3, gg3, state, w2.T.astype(BF16), bg.reshape(GLA_KW, 1), gain)
    return o.reshape(bd, GLA_W), snew


def _head_rows(a, n_heads, width):
    return jnp.concatenate([jnp.broadcast_to(a[h:h + 1, :], (width, 1)) for h in range(n_heads)], axis=0)


def _fox_dec_kernel(pt_ref, q_ref, cnew_ref, knew_ref, vnew_ref, g_ref, u_ref, *rest, gp):
    k_refs = rest[0:gp]
    v_refs = rest[gp:2 * gp]
    lf_refs = rest[2 * gp:3 * gp]
    o_ref, m_ref, l_ref, acc_ref, sfx_ref, qb_ref = rest[3 * gp:]
    j = pl.program_id(1)
    nj = pl.num_programs(1)

    @pl.when(j == 0)
    def _():
        m_ref[...] = jnp.full_like(m_ref, NEG_INF)
        l_ref[...] = jnp.zeros_like(l_ref)
        acc_ref[...] = jnp.zeros_like(acc_ref)
        sfx_ref[...] = jnp.zeros_like(sfx_ref)
        qb_ref[...] = jnp.broadcast_to(q_ref[0], qb_ref.shape)

    cnew = cnew_ref[0]
    carry = sfx_ref[...]
    scores = []
    for g in range(gp):
        lf = lf_refs[g][0]
        hi, lo = _split_bf16(lf)
        sfx = _dot(hi, u_ref[...]) + _dot(lo, u_ref[...]) + carry
        carry = carry + jnp.sum(lf, axis=1, keepdims=True)
        prod = k_refs[g][0] * qb_ref[...]
        s = jnp.sum(prod.reshape(FOX_H, FOX_D, PAGE), axis=1) + sfx + cnew
        scores.append(s)
    sfx_ref[...] = carry
    m_prev = m_ref[...]
    m_new = m_prev
    for s in scores:
        m_new = jnp.maximum(m_new, jnp.max(s, axis=1, keepdims=True))
    alpha = jnp.exp(m_prev - m_new)
    ps = [jnp.exp(s - m_new) for s in scores]
    l_new = alpha * l_ref[...]
    for p in ps:
        l_new = l_new + jnp.sum(p, axis=1, keepdims=True)
    for h in range(FOX_H):
        rows = slice(h * FOX_D, (h + 1) * FOX_D)
        a = acc_ref[rows, :] * alpha[h:h + 1, :]
        for g in range(gp):
            a = a + ps[g][h:h + 1, :] * v_refs[g][0, rows, :]
        acc_ref[rows, :] = a
    m_ref[...] = m_new
    l_ref[...] = l_new

    @pl.when(j == nj - 1)
    def _():
        s_new = jnp.sum((q_ref[0] * knew_ref[0]).reshape(FOX_H, FOX_D, 1), axis=1)
        m_fin = jnp.maximum(m_new, s_new)
        a2 = jnp.exp(m_new - m_fin)
        p_new = jnp.exp(s_new - m_fin)
        l_fin = a2 * l_new + p_new
        tot = jnp.sum(acc_ref[...], axis=1, keepdims=True)
        num = _head_rows(a2, FOX_H, FOX_D) * tot + _head_rows(p_new, FOX_H, FOX_D) * vnew_ref[0]
        o = num / _head_rows(l_fin, FOX_H, FOX_D)
        o_ref[0] = o * _silu(g_ref[0])


def _fox_decode(qn_bf, kn, v, logf8, gate, cache_k, cache_v, cache_lf, layer, page_table, *, gp=8):
    bd = qn_bf.shape[0]
    n_pages = page_table.shape[1]
    n_pool = cache_k.shape[1]
    gp = min(gp, n_pages)
    nj = n_pages // gp
    ck = jnp.transpose(cache_k, (0, 1, 3, 4, 2)).reshape(-1, FOX_W, PAGE)
    cv = jnp.transpose(cache_v, (0, 1, 3, 4, 2)).reshape(-1, FOX_W, PAGE)
    clf = jnp.transpose(cache_lf, (0, 1, 3, 2)).reshape(-1, FOX_H, PAGE)
    base = layer * n_pool
    u_mat = jnp.asarray(np.tril(np.ones((PAGE, PAGE)), -1), BF16)
    col = lambda w: pl.BlockSpec((1, w, 1), lambda b, j, pt: (b, 0, 0))
    full = lambda *s: pl.BlockSpec(s, lambda b, j, pt: (0,) * len(s))

    def page_spec(w, g):
        return pl.BlockSpec((1, w, PAGE),
                            lambda b, j, pt: (base + pt[b, n_pages - 1 - (j * gp + g)], 0, 0))

    in_specs = ([col(FOX_W), col(FOX_H), col(FOX_W), col(FOX_W), col(FOX_W), full(PAGE, PAGE)]
                + [page_spec(FOX_W, g) for g in range(gp)]
                + [page_spec(FOX_W, g) for g in range(gp)]
                + [page_spec(FOX_H, g) for g in range(gp)])
    c3 = lambda a: a.astype(F32).reshape(bd, -1, 1)
    out = pl.pallas_call(
        functools.partial(_fox_dec_kernel, gp=gp),
        grid_spec=pltpu.PrefetchScalarGridSpec(
            num_scalar_prefetch=1,
            grid=(bd, nj),
            in_specs=in_specs,
            out_specs=col(FOX_W),
            scratch_shapes=[pltpu.VMEM((FOX_H, 1), F32), pltpu.VMEM((FOX_H, 1), F32),
                            pltpu.VMEM((FOX_W, PAGE), F32), pltpu.VMEM((FOX_H, 1), F32),
                            pltpu.VMEM((FOX_W, PAGE), F32)]),
        out_shape=jax.ShapeDtypeStruct((bd, FOX_W, 1), F32),
        compiler_params=_cparams(("parallel", "arbitrary")),
        name="fox_decode",
    )(page_table, c3(qn_bf), c3(logf8), c3(kn), c3(v), c3(gate), u_mat,
      *([ck] * gp), *([cv] * gp), *([clf] * gp))
    return out.reshape(bd, FOX_W)


def _mla_dec_kernel(pt_ref, qa_ref, qr_ref, cnew_ref, pnew_ref, wkt_ref, *rest, gp):
    c_refs = rest[0:gp]
    p_refs = rest[gp:2 * gp]
    o_ref, m_ref, l_ref, acc_ref = rest[2 * gp:]
    j = pl.program_id(1)
    nj = pl.num_programs(1)

    @pl.when(j == 0)
    def _():
        m_ref[...] = jnp.full_like(m_ref, NEG_INF)
        l_ref[...] = jnp.zeros_like(l_ref)
        acc_ref[...] = jnp.zeros_like(acc_ref)

    qa = qa_ref[0]
    qr = qr_ref[0]

    def scores_t(cb, pt):
        n = cb.shape[0]
        kn = _dot_nt(wkt_ref[...], cb)
        ssq = jnp.sum((kn * kn).reshape(MLA_H, MLA_NOPE, n), axis=1)
        ssq = ssq + jnp.sum(pt * pt, axis=0, keepdims=True)
        raw = _dot_nt(qa, cb) + _dot(qr, pt.astype(BF16))
        return raw * lax.rsqrt(ssq * (1.0 / MLA_QK) + RMS_EPS)

    scores = []
    cbs = []
    for g in range(0, gp, 2):
        cb = jnp.concatenate([c_refs[g][0].astype(BF16), c_refs[g + 1][0].astype(BF16)], axis=0)
        pt = jnp.concatenate([p_refs[g][0], p_refs[g + 1][0]], axis=1)
        scores.append(scores_t(cb, pt))
        cbs.append(cb)
    m_prev = m_ref[...]
    m_new = m_prev
    for s in scores:
        m_new = jnp.maximum(m_new, jnp.max(s, axis=1, keepdims=True))
    alpha = jnp.exp(m_prev - m_new)
    l_new = alpha * l_ref[...]
    acc = alpha * acc_ref[...]
    for s, cb in zip(scores, cbs):
        p = jnp.exp(s - m_new)
        l_new = l_new + jnp.sum(p, axis=1, keepdims=True)
        acc = acc + _dot(p.astype(BF16), cb)
    m_ref[...] = m_new
    l_ref[...] = l_new
    acc_ref[...] = acc

    @pl.when(j == nj - 1)
    def _():
        cnew = cnew_ref[0]
        cblk = jnp.broadcast_to(cnew, (PAGE, MLA_KV_LORA)).astype(BF16)
        pblk = jnp.broadcast_to(pnew_ref[0], (MLA_ROPE, PAGE))
        s_new = scores_t(cblk, pblk)[:, 0:1]
        m_fin = jnp.maximum(m_new, s_new)
        a2 = jnp.exp(m_new - m_fin)
        p_new = jnp.exp(s_new - m_fin)
        l_fin = a2 * l_new + p_new
        num = a2 * acc + p_new * cnew
        o_ref[0] = num / l_fin


def _mla_decode(qa, qr, c_new, kpe_new, wkt_bf16, cache_c, cache_p, layer, page_table, *, gp=16):
    bd = qa.shape[0]
    n_pages = page_table.shape[1]
    n_pool = cache_c.shape[1]
    gp = min(gp, n_pages)
    nj = n_pages // gp
    cc = cache_c.reshape(-1, PAGE, MLA_KV_LORA)
    cp = jnp.transpose(cache_p, (0, 1, 3, 2)).reshape(-1, MLA_ROPE, PAGE)
    base = layer * n_pool
    full = lambda *s: pl.BlockSpec(s, lambda b, j, pt: (0,) * len(s))
    per_b = lambda *s: pl.BlockSpec((1,) + s, lambda b, j, pt: (b, 0, 0))

    def page_spec(r, c, g):
        return pl.BlockSpec((1, r, c), lambda b, j, pt: (base + pt[b, j * gp + g], 0, 0))

    in_specs = ([per_b(MLA_H, MLA_KV_LORA), per_b(MLA_H, MLA_ROPE), per_b(1, MLA_KV_LORA), per_b(MLA_ROPE, 1),
                 full(MLA_H * MLA_NOPE, MLA_KV_LORA)]
                + [page_spec(PAGE, MLA_KV_LORA, g) for g in range(gp)]
                + [page_spec(MLA_ROPE, PAGE, g) for g in range(gp)])
    lat = pl.pallas_call(
        functools.partial(_mla_dec_kernel, gp=gp),
        grid_spec=pltpu.PrefetchScalarGridSpec(
            num_scalar_prefetch=1,
            grid=(bd, nj),
            in_specs=in_specs,
            out_specs=per_b(MLA_H, MLA_KV_LORA),
            scratch_shapes=[pltpu.VMEM((MLA_H, 1), F32), pltpu.VMEM((MLA_H, 1), F32),
                            pltpu.VMEM((MLA_H, MLA_KV_LORA), F32)]),
        out_shape=jax.ShapeDtypeStruct((bd, MLA_H, MLA_KV_LORA), F32),
        compiler_params=_cparams(("parallel", "arbitrary")),
        name="mla_decode",
    )(page_table, qa, qr, c_new.reshape(bd, 1, MLA_KV_LORA), kpe_new.reshape(bd, MLA_ROPE, 1),
      wkt_bf16, *([cc] * gp), *([cp] * gp))
    return lat


def _latent_out_kernel(lat_ref, wv_ref, g_ref, o_ref):
    for h in range(MLA_H):
        o = _dot(lat_ref[:, h * MLA_KV_LORA:(h + 1) * MLA_KV_LORA].astype(BF16), wv_ref[h])
        sl = slice(h * MLA_V, (h + 1) * MLA_V)
        o_ref[:, sl] = o * _silu(g_ref[:, sl])


def _latent_out(lat, wv_bf16, gate):
    bd = lat.shape[0]
    return pl.pallas_call(
        _latent_out_kernel,
        out_shape=jax.ShapeDtypeStruct((bd, MLA_W), F32),
        compiler_params=pltpu.CompilerParams(vmem_limit_bytes=VMEM_LIMIT),
        name="latent_out",
    )(lat.reshape(bd, MLA_H * MLA_KV_LORA), wv_bf16, gate)


def _out_kernel(*refs, widths):
    n = len(widths)
    part_refs = refs[:n]
    w_ref, x_ref, o_ref = refs[n:]
    acc = x_ref[...]
    off = 0
    for p_ref, wd in zip(part_refs, widths):
        acc = acc + _dot(p_ref[...].astype(BF16), w_ref[off:off + wd, :])
        off += wd
    o_ref[...] = acc


def _out_proj(parts, w_bf16, x, *, tm=512):
    m, d = x.shape
    tm = min(tm, m)
    widths = tuple(p.shape[1] for p in parts)
    row = lambda w: pl.BlockSpec((tm, w), lambda i: (i, 0))
    return pl.pallas_call(
        functools.partial(_out_kernel, widths=widths),
        grid=(m // tm,),
        in_specs=[row(w) for w in widths] + [pl.BlockSpec(w_bf16.shape, lambda i: (0, 0)), row(d)],
        out_specs=row(d),
        out_shape=jax.ShapeDtypeStruct((m, d), F32),
        compiler_params=_cparams(("parallel",)),
        name="out_proj",
    )(*parts, w_bf16, x)


def _rms(x, gain):
    y = x * lax.rsqrt(jnp.mean(x * x, axis=-1, keepdims=True) + RMS_EPS)
    return y * gain


def _rope_tables(pos, reps):
    half = MLA_ROPE // 2
    inv = ROPE_THETA ** (-jnp.arange(half, dtype=F32) / half)
    ang = pos.astype(F32)[:, None] * inv[None, :]
    return jnp.tile(jnp.cos(ang), (1, reps)), jnp.tile(jnp.sin(ang), (1, reps))


def _even_weights(w_in, b_f, q_gain, k_gain, w_g2):
    o = np.cumsum((0, FOX_W, FOX_W, FOX_W, FOX_H, FOX_W, GLA_KW, GLA_KW, GLA_W, GLA_RANK, GLA_W))
    seg = lambda i: w_in[:, o[i]:o[i + 1]]
    pad = jnp.zeros((w_in.shape[0], 128 - FOX_H - GLA_RANK), w_in.dtype)
    w = jnp.concatenate([seg(0), seg(1), seg(2), seg(4), seg(5), seg(6), seg(7), seg(9), seg(3), seg(8), pad],
                        axis=1).astype(BF16)
    bias128 = jnp.zeros((1, 128), F32).at[0, :FOX_H].set(b_f)
    qg = jnp.tile(q_gain, FOX_H).reshape(1, FOX_W)
    kg = jnp.tile(k_gain, FOX_H).reshape(1, FOX_W)
    w2pad = jnp.zeros((128, GLA_KW), F32).at[FOX_H:FOX_H + GLA_RANK].set(w_g2).astype(BF16)
    return w, bias128, qg, kg, w2pad


def _even_layer(xp, xs, e, cache_k, cache_v, cache_lf, state, page_table, norm, w_in, b_f, q_gain, k_gain,
                w_g2, b_g, out_gain, w_out):
    bp, tp, d = xp.shape
    bd = xs.shape[0]
    w, bias128, qg, kg, w2pad = _even_weights(w_in, b_f, q_gain, k_gain, w_g2)
    bg = b_g.reshape(1, GLA_KW)
    gain = out_gain.reshape(1, GLA_DV)
    w_out_bf = w_out.astype(BF16)

    x2 = xp.reshape(bp * tp, d)
    qn, kn, knb, v, vb, rest, misc, cum = _even_in(x2, norm, w, bias128, qg, kg, tp)
    r3 = lambda a: a.reshape(bp, tp, a.shape[-1])
    rest3 = r3(rest)
    ck = r3(cum)[:, :, :FOX_H].transpose(0, 2, 1)
    fox = _flash(r3(qn), r3(knb), r3(vb), rest3, 0, FOX_H, FOX_D, FOX_D, cq=r3(cum), ck=ck)
    gla, sfin = _gla_prompt(rest3, r3(misc), w2pad, bg, gain)
    xp_new = _out_proj([fox.reshape(bp * tp, FOX_W), gla.reshape(bp * tp, GLA_W)], w_out_bf, x2).reshape(bp, tp, d)
    outs_p = (kn.reshape(bp, tp, FOX_H, FOX_D), v.reshape(bp, tp, FOX_H, FOX_D),
              r3(misc)[:, :, :FOX_H], sfin.transpose(0, 1, 3, 2))

    xs2 = xs.reshape(bd, d)
    qn, kn, knb, v, vb, rest, misc, cum = _even_in(xs2, norm, w, bias128, qg, kg, 1)
    fox = _fox_decode(qn, kn, v, misc[:, :FOX_H], rest[:, :FOX_W], cache_k, cache_v, cache_lf, e, page_table)
    gla, snew = _gla_step(rest, misc, state, w_g2, b_g, gain)
    xs_new = _out_proj([fox, gla], w_out_bf, xs2).reshape(bd, 1, d)
    outs_s = (kn.reshape(bd, 1, FOX_H, FOX_D), v.reshape(bd, 1, FOX_H, FOX_D),
              misc[:, :FOX_H].reshape(bd, 1, FOX_H), snew)
    return xp_new, xs_new, outs_p, outs_s


def _odd_weights(w_in, w_q_b, w_kv_b):
    pad = jnp.zeros((w_in.shape[0], 128 - MLA_ROPE), w_in.dtype)
    o = np.cumsum((0, MLA_Q_LORA, MLA_KV_LORA, MLA_ROPE, MLA_W))
    w = jnp.concatenate([w_in[:, o[0]:o[1]], w_in[:, o[1]:o[2]], w_in[:, o[3]:o[4]], w_in[:, o[2]:o[3]], pad],
                        axis=1).astype(BF16)
    half = MLA_ROPE // 2
    wq = w_q_b.reshape(MLA_Q_LORA, MLA_H, MLA_QK)
    wq = jnp.concatenate([wq[:, :, :MLA_NOPE].reshape(MLA_Q_LORA, -1),
                          wq[:, :, MLA_NOPE:MLA_NOPE + half].reshape(MLA_Q_LORA, -1),
                          wq[:, :, MLA_NOPE + half:].reshape(MLA_Q_LORA, -1)], axis=1).astype(BF16)
    wkv = w_kv_b.reshape(MLA_KV_LORA, MLA_H, MLA_NOPE + MLA_V)
    wk = wkv[:, :, :MLA_NOPE].reshape(MLA_KV_LORA, -1)
    wv = wkv[:, :, MLA_NOPE:].reshape(MLA_KV_LORA, -1)
    return w, wq, wk, wv


def _mla_queries(z, pos, q_a_gain, wq, q_gain):
    m = z.shape[0]
    half = MLA_ROPE // 2
    q = _norm_matmul(z[:, :MLA_Q_LORA], q_a_gain, wq)
    nope = q[:, :MLA_H * MLA_NOPE].reshape(m, MLA_H, MLA_NOPE)
    x1 = q[:, MLA_H * MLA_NOPE:MLA_H * (MLA_NOPE + half)].reshape(m, MLA_H, half)
    x2 = q[:, MLA_H * (MLA_NOPE + half):].reshape(m, MLA_H, half)
    cos, sin = _rope_tables(pos, 1)
    cos, sin = cos[:, None, :], sin[:, None, :]
    qf = jnp.concatenate([nope, x1 * cos - x2 * sin, x2 * cos + x1 * sin], axis=-1)
    return _rms(qf, q_gain)


def _mla_latents(z, pos, kv_a_gain):
    half = MLA_ROPE // 2
    c = _rms(z[:, MLA_Q_LORA:MLA_Q_LORA + MLA_KV_LORA], kv_a_gain)
    o = MLA_Q_LORA + MLA_KV_LORA + MLA_W
    kp = z[:, o:o + MLA_ROPE]
    cos, sin = _rope_tables(pos, 1)
    x1, x2 = kp[:, :half], kp[:, half:]
    kpe = jnp.concatenate([x1 * cos - x2 * sin, x2 * cos + x1 * sin], axis=-1)
    return c, kpe


def _odd_layer(xp, xs, o, cache_c, cache_p, page_table, norm, w_in, q_a_gain, w_q_b, kv_a_gain, w_kv_b,
               q_gain, k_gain, w_out):
    bp, tp, d = xp.shape
    bd = xs.shape[0]
    past = page_table.shape[1] * PAGE
    w, wq, wk, wv = _odd_weights(w_in, w_q_b, w_kv_b)
    w_out_bf = w_out.astype(BF16)
    g_lo = MLA_Q_LORA + MLA_KV_LORA

    x2 = xp.reshape(bp * tp, d)
    pos_p = jnp.tile(jnp.arange(tp), bp)
    z = _norm_matmul(x2, norm, w)
    qf = _mla_queries(z, pos_p, q_a_gain, wq, q_gain) * MLA_SCALE
    c, kpe = _mla_latents(z, pos_p, kv_a_gain)
    kv = _norm_matmul(c, kv_a_gain, jnp.concatenate([wk, wv], axis=1).astype(BF16), norm=False)
    k_nope = kv[:, :MLA_H * MLA_NOPE].reshape(-1, MLA_H, MLA_NOPE)
    kf = jnp.concatenate([k_nope, jnp.broadcast_to(kpe[:, None, :], (bp * tp, MLA_H, MLA_ROPE))], axis=-1)
    kf = _rms(kf, k_gain)
    padh = lambda a: jnp.pad(a, ((0, 0), (0, 0), (0, MLA_HP - MLA_QK))).astype(BF16).reshape(bp, tp, MLA_H * MLA_HP)
    vb = kv[:, MLA_H * MLA_NOPE:].astype(BF16).reshape(bp, tp, MLA_W)
    gate = z[:, g_lo:g_lo + MLA_W].reshape(bp, tp, MLA_W)
    att = _flash(padh(qf), padh(kf), vb, gate, 0, MLA_H, MLA_HP, MLA_V)
    xp_new = _out_proj([att.reshape(bp * tp, MLA_W)], w_out_bf, x2).reshape(bp, tp, d)
    outs_p = (c.reshape(bp, tp, MLA_KV_LORA), kpe.reshape(bp, tp, MLA_ROPE))

    xs2 = xs.reshape(bd, d)
    pos_s = jnp.full((bd,), past, jnp.int32)
    z = _norm_matmul(xs2, norm, w)
    qf = _mla_queries(z, pos_s, q_a_gain, wq, q_gain) * MLA_SCALE
    c, kpe = _mla_latents(z, pos_s, kv_a_gain)
    qk = qf * k_gain
    wk3 = wk.reshape(MLA_KV_LORA, MLA_H, MLA_NOPE)
    qa = jnp.einsum('bhd,rhd->bhr', qk[:, :, :MLA_NOPE], wk3).astype(BF16)
    qr = qk[:, :, MLA_NOPE:].astype(BF16)
    lat = _mla_decode(qa, qr, c, kpe, wk.T.astype(BF16), cache_c, cache_p, o, page_table)
    wv3 = wv.reshape(MLA_KV_LORA, MLA_H, MLA_V).transpose(1, 0, 2).astype(BF16)
    att = _latent_out(lat, wv3, z[:, g_lo:g_lo + MLA_W])
    xs_new = _out_proj([att], w_out_bf, xs2).reshape(bd, 1, d)
    outs_s = (c.reshape(bd, 1, MLA_KV_LORA), kpe.reshape(bd, 1, MLA_ROPE))
    return xp_new, xs_new, outs_p, outs_s


def kernel(x_prompt, x_sample, cache_fox_k, cache_fox_v, cache_fox_logf, cache_mla_ckv, cache_mla_kpe,
           state_gla, page_table, norm_even, w_in_even, b_fox_f, fox_q_gain, fox_k_gain, gla_w_gate2,
           gla_b_gate, gla_out_gain, w_out_even, norm_odd, w_in_odd, mla_q_a_gain, w_q_b, mla_kv_a_gain,
           w_kv_b, mla_q_gain, mla_k_gain, w_out_odd):
    depth = norm_even.shape[0] + norm_odd.shape[0]
    xp, xs = x_prompt, x_sample
    ev_p, ev_s, od_p, od_s = [], [], [], []
    for layer in range(depth):
        i = layer // 2
        if layer % 2 == 0:
            xp, xs, op, os_ = _even_layer(
                xp, xs, i, cache_fox_k, cache_fox_v, cache_fox_logf, state_gla[i], page_table,
                norm_even[i], w_in_even[i], b_fox_f[i], fox_q_gain[i], fox_k_gain[i], gla_w_gate2[i],
                gla_b_gate[i], gla_out_gain[i], w_out_even[i])
            ev_p.append(op)
            ev_s.append(os_)
        else:
            xp, xs, op, os_ = _odd_layer(
                xp, xs, i, cache_mla_ckv, cache_mla_kpe, page_table, norm_odd[i], w_in_odd[i],
                mla_q_a_gain[i], w_q_b[i], mla_kv_a_gain[i], w_kv_b[i], mla_q_gain[i], mla_k_gain[i],
                w_out_odd[i])
            od_p.append(op)
            od_s.append(os_)
    st = lambda lst, k: jnp.stack([t[k] for t in lst])
    return (xp, xs,
            st(ev_p, 0), st(ev_p, 1), st(ev_p, 2), st(od_p, 0), st(od_p, 1), st(ev_p, 3),
            st(ev_s, 0), st(ev_s, 1), st(ev_s, 2), st(od_s, 0), st(od_s, 1), st(ev_s, 3))
```

```python
import functools

import numpy as np
import jax
import jax.numpy as jnp
from jax import lax
from jax.experimental import pallas as pl
from jax.experimental.pallas import tpu as pltpu

F32 = jnp.float32
BF16 = jnp.bfloat16

D_MODEL = 1024
PAGE = 128
RMS_EPS = 1e-6
NEG_INF = -1e30
FOX_H = 8
FOX_D = 64
FOX_W = FOX_H * FOX_D
FOX_SCALE = FOX_D ** -0.5
GLA_H = 4
GLA_DK = 64
GLA_DV = 128
GLA_KW = GLA_H * GLA_DK
GLA_W = GLA_H * GLA_DV
GLA_RANK = 16
GLA_GATE_NORM = 16.0
GLA_CHUNK = 64
MLA_H = 16
MLA_Q_LORA = 256
MLA_KV_LORA = 128
MLA_NOPE = 64
MLA_ROPE = 32
MLA_QK = MLA_NOPE + MLA_ROPE
MLA_V = 64
MLA_W = MLA_H * MLA_V
MLA_SCALE = MLA_QK ** -0.5
HP = 128
ROPE_THETA = 10000.0
LOG2E = 1.4426950408889634

VMEM_LIMIT = 56 * 1024 * 1024

E_Q, E_K, E_V, E_REST, E_MISC, E_END = 0, 512, 1024, 1536, 3584, 3712


def _cparams(sem):
    return pltpu.CompilerParams(dimension_semantics=sem, vmem_limit_bytes=VMEM_LIMIT)


def _log_sigmoid(x):
    return jnp.minimum(x, 0.0) - jnp.log1p(jnp.exp(-jnp.abs(x)))


def _silu(x):
    return x / (1.0 + jnp.exp(-x))


def _split_bf16(x):
    hi = x.astype(BF16)
    lo = (x - hi.astype(F32)).astype(BF16)
    return hi, lo


def _dot(a, b):
    return jnp.dot(a, b, preferred_element_type=F32)


def _dot_nt(a, b):
    return lax.dot_general(a, b, (((1,), (1,)), ((), ())), preferred_element_type=F32)


def _split3_bf16(x):
    hi = x.astype(BF16)
    r = x - hi.astype(F32)
    mid = r.astype(BF16)
    lo = (r - mid.astype(F32)).astype(BF16)
    return hi, mid, lo


def _rmsnorm(x, gain):
    return x * lax.rsqrt(jnp.mean(x * x, axis=-1, keepdims=True) + RMS_EPS) * gain


O_QA, O_KVA, O_KPE, O_G, O_END = 0, 256, 384, 512, 1536
ROPE_LO = MLA_NOPE
ROPE_HALF = MLA_ROPE // 2


def _odd_in_kernel(x_ref, g_ref, w_ref, qag_ref, wq_ref, kvg_ref, wk_ref, wv_ref, qg_ref, kg_ref, ones_ref,
                   cos_ref, sa_ref, sb_ref, q_ref, k_ref, vt_ref, c_ref, kpe_ref, gate_ref, *, q_scale):
    xn = _rmsnorm(x_ref[...], g_ref[...]).astype(BF16)

    def mm(lo, hi):
        return _dot(xn, w_ref[:, lo:hi])

    cos, sa, sb = cos_ref[...], sa_ref[...], sb_ref[...]

    def rope(t):
        return (t * cos + pltpu.roll(t, ROPE_HALF, 1) * sa + pltpu.roll(t, HP - ROPE_HALF, 1) * sb)

    def headnorm(t, gain):
        ss = _dot((t * t).astype(BF16), ones_ref[...])
        return t * lax.rsqrt(ss * (1.0 / MLA_QK) + RMS_EPS) * gain

    gate_ref[...] = mm(O_G, O_END)
    c = _rmsnorm(mm(O_KVA, O_KPE), kvg_ref[...])
    c_ref[...] = c
    kpe_t = rope(mm(O_KPE, O_G))
    kpe_ref[...] = kpe_t[:, ROPE_LO:ROPE_LO + MLA_ROPE]
    qa = _rmsnorm(mm(O_QA, O_KVA), qag_ref[...]).astype(BF16)
    cb = c.astype(BF16)
    for h in range(MLA_H):
        sl = slice(h * HP, (h + 1) * HP)
        qt = rope(_dot(qa, wq_ref[:, sl]))
        q_ref[:, sl] = (headnorm(qt, qg_ref[...]) * q_scale).astype(BF16)
        kt = _dot(cb, wk_ref[:, sl]) + kpe_t
        k_ref[:, sl] = headnorm(kt, kg_ref[...]).astype(BF16)
    vt_ref[...] = _dot(cb, wv_ref[...]).T.astype(BF16)


def _odd_in(x, gain, w, qag, wq, kvg, wk, wv, qg, kg, tables, q_scale, *, tm=256):
    m, d = x.shape
    tm = min(tm, m)
    cos_t, sa_t, sb_t = tables
    n_tab = cos_t.shape[0] // tm
    row = lambda wd: pl.BlockSpec((tm, wd), lambda i: (i, 0))
    full = lambda a, b: pl.BlockSpec((a, b), lambda i: (0, 0))
    tab = pl.BlockSpec((tm, HP), lambda i: (i % n_tab, 0))
    ones = jnp.ones((HP, HP), BF16)
    return pl.pallas_call(
        functools.partial(_odd_in_kernel, q_scale=q_scale),
        grid=(m // tm,),
        in_specs=[row(d), full(1, d), full(d, O_END), full(1, MLA_Q_LORA), full(MLA_Q_LORA, MLA_H * HP),
                  full(1, MLA_KV_LORA), full(MLA_KV_LORA, MLA_H * HP), full(MLA_KV_LORA, MLA_W),
                  full(1, HP), full(1, HP), full(HP, HP), tab, tab, tab],
        out_specs=[row(MLA_H * HP), row(MLA_H * HP), pl.BlockSpec((MLA_W, tm), lambda i: (0, i)),
                   row(MLA_KV_LORA), row(MLA_ROPE), row(MLA_W)],
        out_shape=[jax.ShapeDtypeStruct((m, MLA_H * HP), BF16),
                   jax.ShapeDtypeStruct((m, MLA_H * HP), BF16),
                   jax.ShapeDtypeStruct((MLA_W, m), BF16),
                   jax.ShapeDtypeStruct((m, MLA_KV_LORA), F32),
                   jax.ShapeDtypeStruct((m, MLA_ROPE), F32),
                   jax.ShapeDtypeStruct((m, MLA_W), F32)],
        compiler_params=_cparams(("parallel",)),
        name="odd_in_proj",
    )(x, gain.reshape(1, d), w, qag.reshape(1, -1), wq, kvg.reshape(1, -1), wk, wv, qg, kg, ones,
      cos_t, sa_t, sb_t)


AUG_Q = FOX_D
AUG_K = FOX_D + 3


def _even_in_kernel(x_ref, g_ref, w_ref, bias_ref, qg_ref, kg_ref, ones_ref, tri_ref, place_ref, pc_ref, aug_ref,
                    qa_ref, ka_ref, kn_ref, v_ref, vt_ref, rest_ref, misc_ref,
                    carry_ref, *, tiles_per_seq, do_cum, q_scale):
    xn = _rmsnorm(x_ref[...], g_ref[...]).astype(BF16)

    def mm(lo, hi):
        return _dot(xn, w_ref[:, lo:hi])

    def headnorm(z, gain):
        ss = _dot((z * z).astype(BF16), ones_ref[...])
        return z * lax.rsqrt(ss * (1.0 / FOX_D) + RMS_EPS) * gain

    q = headnorm(mm(E_Q, E_K), qg_ref[...]) * q_scale
    k = headnorm(mm(E_K, E_V), kg_ref[...])
    kn_ref[...] = k
    v = mm(E_V, E_REST)
    v_ref[...] = v
    vt_ref[...] = v.T.astype(BF16)
    rest_ref[...] = mm(E_REST, E_MISC)
    mz = mm(E_MISC, E_END) + bias_ref[...]
    lane = lax.broadcasted_iota(jnp.int32, mz.shape, 1)
    logf = jnp.where(lane < FOX_H, _log_sigmoid(mz), mz)
    misc_ref[...] = logf
    qa = _dot(q.astype(BF16), place_ref[...])
    ka = _dot(k.astype(BF16), place_ref[...])
    if do_cum:
        @pl.when(pl.program_id(0) % tiles_per_seq == 0)
        def _():
            carry_ref[...] = jnp.zeros_like(carry_ref)

        lf = jnp.where(lane < FOX_H, logf, 0.0)
        hi, lo = _split_bf16(lf)
        c = _dot(tri_ref[...], hi) + _dot(tri_ref[...], lo) + carry_ref[...]
        carry_ref[...] = c[c.shape[0] - 1:, :]
        pieces = _split3_bf16(c * LOG2E)
        for p in range(3):
            qa = qa + _dot(pieces[p], pc_ref[p])
            ka = ka + _dot(pieces[p], pc_ref[3 + p])
        qa = qa + aug_ref[0:1, :]
        ka = ka + aug_ref[1:2, :]
    qa_ref[...] = qa.astype(BF16)
    ka_ref[...] = ka.astype(BF16)


def _even_in(x, gain, w_bf16, bias128, qg, kg, seq_len, q_scale, *, tm=256):
    m, k = x.shape
    tm = min(tm, m, seq_len) if seq_len > 1 else min(tm, m)
    do_cum = seq_len > 1
    tiles_per_seq = max(seq_len // tm, 1)
    ones_bd = jnp.asarray(np.kron(np.eye(FOX_H), np.ones((FOX_D, FOX_D))), BF16)
    tri = jnp.asarray(np.tril(np.ones((tm, tm))), BF16)
    place = np.zeros((FOX_H, FOX_D, FOX_H, HP), np.float32)
    pc = np.zeros((6, 128, FOX_H, HP), np.float32)
    aug = np.zeros((2, FOX_H, HP), np.float32)
    for h in range(FOX_H):
        place[h, np.arange(FOX_D), h, np.arange(FOX_D)] = 1.0
        for p in range(3):
            pc[p, h, h, AUG_Q + p] = 1.0
            pc[3 + p, h, h, AUG_K + p] = -1.0
        aug[0, h, AUG_K:AUG_K + 3] = 1.0
        aug[1, h, AUG_Q:AUG_Q + 3] = 1.0
    place = jnp.asarray(place.reshape(FOX_W, FOX_H * HP), BF16)
    pc = jnp.asarray(pc.reshape(6, 128, FOX_H * HP), BF16)
    aug = jnp.asarray(aug.reshape(2, FOX_H * HP), F32)
    row = lambda w: pl.BlockSpec((tm, w), lambda i: (i, 0))
    full = lambda *s: pl.BlockSpec(s, lambda i: (0,) * len(s))
    outs = pl.pallas_call(
        functools.partial(_even_in_kernel, tiles_per_seq=tiles_per_seq, do_cum=do_cum, q_scale=q_scale),
        grid=(m // tm,),
        in_specs=[row(k), full(1, k), full(k, E_END), full(1, 128), full(1, FOX_W), full(1, FOX_W),
                  full(FOX_W, FOX_W), full(tm, tm), full(FOX_W, FOX_H * HP), full(6, 128, FOX_H * HP),
                  full(2, FOX_H * HP)],
        out_specs=[row(FOX_H * HP), row(FOX_H * HP), row(FOX_W), row(FOX_W),
                   pl.BlockSpec((FOX_W, tm), lambda i: (0, i)),
                   row(E_MISC - E_REST), row(128)],
        out_shape=[jax.ShapeDtypeStruct((m, FOX_H * HP), BF16),
                   jax.ShapeDtypeStruct((m, FOX_H * HP), BF16),
                   jax.ShapeDtypeStruct((m, FOX_W), F32),
                   jax.ShapeDtypeStruct((m, FOX_W), F32),
                   jax.ShapeDtypeStruct((FOX_W, m), BF16),
                   jax.ShapeDtypeStruct((m, E_MISC - E_REST), F32),
                   jax.ShapeDtypeStruct((m, 128), F32)],
        scratch_shapes=[pltpu.VMEM((1, 128), F32)],
        compiler_params=_cparams(("arbitrary",)),
        name="even_in_proj",
    )(x, gain.reshape(1, k), w_bf16, bias128, qg, kg, ones_bd, tri, place, pc, aug)
    return outs


def _flash_kernel(q_ref, k_ref, vt_ref, g_ref, o_ref, m_ref, l_ref, acc_ref, *, n_heads, dv):
    i = pl.program_id(1)
    j = pl.program_id(2)

    @pl.when(j == 0)
    def _():
        m_ref[...] = jnp.full_like(m_ref, NEG_INF)
        l_ref[...] = jnp.zeros_like(l_ref)
        acc_ref[...] = jnp.zeros_like(acc_ref)

    def step(masked):
        tq = q_ref.shape[1]
        tk = k_ref.shape[1]
        if masked:
            r = lax.broadcasted_iota(jnp.int32, (tk, tq), 0)
            c = lax.broadcasted_iota(jnp.int32, (tk, tq), 1)
            keep = c >= r
        for h in range(n_heads):
            q = q_ref[0, :, h * HP:(h + 1) * HP]
            k = k_ref[0, :, h * HP:(h + 1) * HP]
            s = _dot_nt(k, q)
            if masked:
                s = jnp.where(keep, s, NEG_INF)
            m_prev = m_ref[h]
            m_new = jnp.maximum(m_prev, jnp.max(s, axis=0, keepdims=True))
            alpha = jnp.exp2(m_prev - m_new)
            p = jnp.exp2(s - m_new)
            l_ref[h] = alpha * l_ref[h] + jnp.sum(p, axis=0, keepdims=True)
            acc_ref[h] = alpha * acc_ref[h] + _dot(vt_ref[h * dv:(h + 1) * dv, :], p.astype(BF16))
            m_ref[h] = m_new

    @pl.when(j < i)
    def _():
        step(False)

    @pl.when(j == i)
    def _():
        step(True)
        for h in range(0, n_heads, 2):
            sl = slice(h * dv, (h + 2) * dv)
            o = jnp.concatenate([acc_ref[h] / l_ref[h], acc_ref[h + 1] / l_ref[h + 1]], axis=0)
            o_ref[0, :, sl] = (o.T * _silu(g_ref[0, :, sl])).astype(o_ref.dtype)


def _flash(q, k, vt, gate, gate_block, n_heads, dv, name, *, t_blk=512):
    b, t, _ = q.shape
    tb = min(t_blk, t)
    n = t // tb
    qmap = lambda bi, i, j: (bi, i, 0)
    kmap = lambda bi, i, j: (bi, jnp.minimum(i, j), 0)
    in_specs = [pl.BlockSpec((1, tb, n_heads * HP), qmap),
                pl.BlockSpec((1, tb, n_heads * HP), kmap),
                pl.BlockSpec((n_heads * dv, tb), lambda bi, i, j: (0, bi * n + jnp.minimum(i, j))),
                pl.BlockSpec((1, tb, n_heads * dv), lambda bi, i, j: (bi, i, gate_block))]
    return pl.pallas_call(
        functools.partial(_flash_kernel, n_heads=n_heads, dv=dv),
        grid=(b, n, n),
        in_specs=in_specs,
        out_specs=pl.BlockSpec((1, tb, n_heads * dv), qmap),
        out_shape=jax.ShapeDtypeStruct((b, t, n_heads * dv), BF16),
        scratch_shapes=[pltpu.VMEM((n_heads, 1, tb), F32),
                        pltpu.VMEM((n_heads, 1, tb), F32),
                        pltpu.VMEM((n_heads, dv, tb), F32)],
        compiler_params=_cparams(("parallel", "parallel", "arbitrary")),
        name=name,
    )(q, k, vt, gate)


def _gla_gate(misc, w2_ref, bg_ref):
    pre = _dot(misc.astype(BF16), w2_ref[...]) + bg_ref[...]
    return _log_sigmoid(pre) * (1.0 / GLA_GATE_NORM)


def _gla_kernel(q_ref, k_ref, v_ref, gg_ref, misc_ref, w2_ref, bg_ref, gain_ref, hsel_ref, tri_ref,
                o_ref, sfin_ref, st_ref, cum_s, k_s, w_s):
    c = pl.program_id(1)
    n_c = pl.num_programs(1)
    ch = GLA_CHUNK

    @pl.when(c == 0)
    def _():
        st_ref[...] = jnp.zeros_like(st_ref)

    q = q_ref[0] * (GLA_DK ** -0.5)
    k = k_ref[0]
    v = v_ref[0]
    la = _gla_gate(misc_ref[0], w2_ref, bg_ref)
    hi, lo = _split_bf16(la)
    cum = _dot(tri_ref[...], hi) + _dot(tri_ref[...], lo)
    cum_s[...] = cum
    k_s[...] = k

    def body(j, carry):
        rowc = cum_s[pl.ds(j, 1), :]
        rowk = k_s[pl.ds(j, 1), :]
        w = (jnp.exp(jnp.minimum(cum - rowc, 0.0)) * (q * rowk)).astype(BF16)
        off = pl.multiple_of(j * 128, 128)
        w_s[0, :, pl.ds(off, 128)] = w[:, 0:128]
        w_s[1, :, pl.ds(off, 128)] = w[:, 128:256]
        return carry

    lax.fori_loop(0, ch, body, 0, unroll=8)
    r = lax.broadcasted_iota(jnp.int32, (ch, ch), 0)
    cc = lax.broadcasted_iota(jnp.int32, (ch, ch), 1)
    causal = r >= cc
    for lt in range(2):
        a2 = _dot(w_s[lt], hsel_ref[...])
        for hh in range(2):
            h = 2 * lt + hh
            sk = slice(h * GLA_DK, (h + 1) * GLA_DK)
            sv = slice(h * GLA_DV, (h + 1) * GLA_DV)
            a = jnp.where(causal, a2[:, hh * ch:(hh + 1) * ch], 0.0)
            vh = v[:, sv]
            vb = vh.astype(BF16)
            cum_h = cum[:, sk]
            last = cum_h[ch - 1:, :]
            st = st_ref[h]
            qe = (q[:, sk] * jnp.exp(cum_h)).astype(BF16)
            o = _dot_nt(qe, st.astype(BF16)) + _dot(a.astype(BF16), vb)
            kd = (k[:, sk] * jnp.exp(last - cum_h)).astype(BF16)
            st_ref[h] = st * jnp.exp(last) + _dot(vh.T.astype(BF16), kd)
            ms = jnp.mean(o * o, axis=-1, keepdims=True)
            y = o * lax.rsqrt(ms + RMS_EPS) * gain_ref[...] * _silu(gg_ref[0, :, sv])
            o_ref[0, :, sv] = y.astype(o_ref.dtype)

    @pl.when(c == n_c - 1)
    def _():
        sfin_ref[0] = st_ref[...]


def _gla_prompt(rest, misc, w2pad, bg, gain):
    b, t, _ = rest.shape
    ch = GLA_CHUNK
    n = t // ch
    hsel = np.zeros((ch, 2, GLA_DK, 2, ch), np.float32)
    for hh in range(2):
        for j in range(ch):
            hsel[j, hh, :, hh, j] = 1.0
    hsel = jnp.asarray(hsel.reshape(ch * 128, 128), BF16)
    tri = jnp.asarray(np.tril(np.ones((ch, ch))), BF16)
    full = lambda *s: pl.BlockSpec(s, lambda bi, ci: (0,) * len(s))
    o, sfin = pl.pallas_call(
        _gla_kernel,
        grid=(b, n),
        in_specs=[pl.BlockSpec((1, ch, GLA_KW), lambda bi, ci: (bi, ci, 2)),
                  pl.BlockSpec((1, ch, GLA_KW), lambda bi, ci: (bi, ci, 3)),
                  pl.BlockSpec((1, ch, GLA_W), lambda bi, ci: (bi, ci, 2)),
                  pl.BlockSpec((1, ch, GLA_W), lambda bi, ci: (bi, ci, 3)),
                  pl.BlockSpec((1, ch, 128), lambda bi, ci: (bi, ci, 0)),
                  full(128, GLA_KW), full(1, GLA_KW), full(1, GLA_DV), full(ch * 128, 128), full(ch, ch)],
        out_specs=[pl.BlockSpec((1, ch, GLA_W), lambda bi, ci: (bi, ci, 0)),
                   pl.BlockSpec((1, GLA_H, GLA_DV, GLA_DK), lambda bi, ci: (bi, 0, 0, 0))],
        out_shape=[jax.ShapeDtypeStruct((b, t, GLA_W), BF16),
                   jax.ShapeDtypeStruct((b, GLA_H, GLA_DV, GLA_DK), F32)],
        scratch_shapes=[pltpu.VMEM((GLA_H, GLA_DV, GLA_DK), F32),
                        pltpu.VMEM((ch, GLA_KW), F32),
                        pltpu.VMEM((ch, GLA_KW), F32),
                        pltpu.VMEM((2, ch, ch * 128), BF16)],
        compiler_params=_cparams(("parallel", "arbitrary")),
        name="gla_prompt",
    )(rest, rest, rest, rest, misc, w2pad, bg, gain, hsel, tri)
    return o, sfin


def _gla_step_kernel(qT_ref, kT_ref, gaT_ref, v_ref, gg_ref, s_ref, w2T_ref, bgT_ref, gain_ref,
                     o_ref, snew_ref, *, tb):
    laT = _log_sigmoid(_dot(w2T_ref[...], gaT_ref[0].astype(BF16)) + bgT_ref[...]) * (1.0 / GLA_GATE_NORM)
    eT = jnp.exp(laT)
    for bb in range(tb):
        for h in range(GLA_H):
            sk = slice(h * GLA_DK, (h + 1) * GLA_DK)
            sv = slice(h * GLA_DV, (h + 1) * GLA_DV)
            s_new = s_ref[bb, h] * eT[sk, bb:bb + 1] + kT_ref[0, sk, bb:bb + 1] * v_ref[bb, :, sv]
            snew_ref[bb, h] = s_new
            qcol = qT_ref[0, sk, bb:bb + 1] * (GLA_DK ** -0.5)
            o = jnp.sum(qcol * s_new, axis=0, keepdims=True)
            ms = jnp.mean(o * o, axis=-1, keepdims=True)
            y = o * lax.rsqrt(ms + RMS_EPS) * gain_ref[...] * _silu(gg_ref[bb, :, sv])
            o_ref[bb, :, sv] = y


def _gla_step(rest, misc, state, w2, bg, gain, *, tb=8):
    bd = rest.shape[0]
    tb = min(tb, bd)
    g = bd // tb
    tr = lambda a: a.reshape(g, tb, a.shape[-1]).transpose(0, 2, 1)
    qT = tr(rest[:, 512:768])
    kT = tr(rest[:, 768:1024])
    gaT = tr(misc[:, 8:8 + GLA_RANK])
    v3 = rest[:, 1024:1536].reshape(bd, 1, GLA_W)
    gg3 = rest[:, 1536:2048].reshape(bd, 1, GLA_W)
    blk = lambda *s: pl.BlockSpec(s, lambda i: (i,) + (0,) * (len(s) - 1))
    full = lambda *s: pl.BlockSpec(s, lambda i: (0,) * len(s))
    o, snew = pl.pallas_call(
        functools.partial(_gla_step_kernel, tb=tb),
        grid=(g,),
        in_specs=[blk(1, GLA_KW, tb), blk(1, GLA_KW, tb), blk(1, GLA_RANK, tb),
                  blk(tb, 1, GLA_W), blk(tb, 1, GLA_W), blk(tb, GLA_H, GLA_DK, GLA_DV),
                  full(GLA_KW, GLA_RANK), full(GLA_KW, 1), full(1, GLA_DV)],
        out_specs=[blk(tb, 1, GLA_W), blk(tb, GLA_H, GLA_DK, GLA_DV)],
        out_shape=[jax.ShapeDtypeStruct((bd, 1, GLA_W), F32),
                   jax.ShapeDtypeStruct((bd, GLA_H, GLA_DK, GLA_DV), F32)],
        compiler_params=_cparams(("parallel",)),
        name="gla_step",
    )(qT, kT, gaT, v3, gg3, state, w2.T.astype(BF16), bg.reshape(GLA_KW, 1), gain)
    return o.reshape(bd, GLA_W), snew


def _head_rows(a, n_heads, width):
    return jnp.concatenate([jnp.broadcast_to(a[h:h + 1, :], (width, 1)) for h in range(n_heads)], axis=0)


def _fox_dec_kernel(pt_ref, q_ref, cnew_ref, knew_ref, vnew_ref, g_ref, u_ref, *rest, gp):
    k_refs = rest[0:gp]
    v_refs = rest[gp:2 * gp]
    lf_refs = rest[2 * gp:3 * gp]
    o_ref, m_ref, l_ref, acc_ref, sfx_ref, qb_ref = rest[3 * gp:]
    j = pl.program_id(1)
    nj = pl.num_programs(1)

    @pl.when(j == 0)
    def _():
        m_ref[...] = jnp.full_like(m_ref, NEG_INF)
        l_ref[...] = jnp.zeros_like(l_ref)
        acc_ref[...] = jnp.zeros_like(acc_ref)
        sfx_ref[...] = jnp.zeros_like(sfx_ref)
        qb_ref[...] = jnp.broadcast_to(q_ref[0], qb_ref.shape)

    cnew = cnew_ref[0]
    carry = sfx_ref[...]
    scores = []
    for g in range(gp):
        lf = lf_refs[g][0]
        hi, lo = _split_bf16(lf)
        sfx = _dot(hi, u_ref[...]) + _dot(lo, u_ref[...]) + carry
        carry = carry + jnp.sum(lf, axis=1, keepdims=True)
        prod = k_refs[g][0] * qb_ref[...]
        s = jnp.sum(prod.reshape(FOX_H, FOX_D, PAGE), axis=1) + sfx + cnew
        scores.append(s)
    sfx_ref[...] = carry
    m_prev = m_ref[...]
    m_new = m_prev
    for s in scores:
        m_new = jnp.maximum(m_new, jnp.max(s, axis=1, keepdims=True))
    alpha = jnp.exp(m_prev - m_new)
    ps = [jnp.exp(s - m_new) for s in scores]
    l_new = alpha * l_ref[...]
    for p in ps:
        l_new = l_new + jnp.sum(p, axis=1, keepdims=True)
    for h in range(FOX_H):
        rows = slice(h * FOX_D, (h + 1) * FOX_D)
        a = acc_ref[rows, :] * alpha[h:h + 1, :]
        for g in range(gp):
            a = a + ps[g][h:h + 1, :] * v_refs[g][0, rows, :]
        acc_ref[rows, :] = a
    m_ref[...] = m_new
    l_ref[...] = l_new

    @pl.when(j == nj - 1)
    def _():
        s_new = jnp.sum((q_ref[0] * knew_ref[0]).reshape(FOX_H, FOX_D, 1), axis=1)
        m_fin = jnp.maximum(m_new, s_new)
        a2 = jnp.exp(m_new - m_fin)
        p_new = jnp.exp(s_new - m_fin)
        l_fin = a2 * l_new + p_new
        tot = jnp.sum(acc_ref[...], axis=1, keepdims=True)
        num = _head_rows(a2, FOX_H, FOX_D) * tot + _head_rows(p_new, FOX_H, FOX_D) * vnew_ref[0]
        o = num / _head_rows(l_fin, FOX_H, FOX_D)
        o_ref[0] = o * _silu(g_ref[0])


def _fox_decode(qn_bf, kn, v, logf8, gate, cache_k, cache_v, cache_lf, layer, page_table, *, gp=16):
    bd = qn_bf.shape[0]
    n_pages = page_table.shape[1]
    n_pool = cache_k.shape[1]
    gp = min(gp, n_pages)
    nj = n_pages // gp
    ck = jnp.transpose(cache_k, (0, 1, 3, 4, 2)).reshape(-1, FOX_W, PAGE)
    cv = jnp.transpose(cache_v, (0, 1, 3, 4, 2)).reshape(-1, FOX_W, PAGE)
    clf = jnp.transpose(cache_lf, (0, 1, 3, 2)).reshape(-1, FOX_H, PAGE)
    base = layer * n_pool
    u_mat = jnp.asarray(np.tril(np.ones((PAGE, PAGE)), -1), BF16)
    col = lambda w: pl.BlockSpec((1, w, 1), lambda b, j, pt: (b, 0, 0))
    full = lambda *s: pl.BlockSpec(s, lambda b, j, pt: (0,) * len(s))

    def page_spec(w, g):
        return pl.BlockSpec((1, w, PAGE),
                            lambda b, j, pt: (base + pt[b, n_pages - 1 - (j * gp + g)], 0, 0))

    in_specs = ([col(FOX_W), col(FOX_H), col(FOX_W), col(FOX_W), col(FOX_W), full(PAGE, PAGE)]
                + [page_spec(FOX_W, g) for g in range(gp)]
                + [page_spec(FOX_W, g) for g in range(gp)]
                + [page_spec(FOX_H, g) for g in range(gp)])
    c3 = lambda a: a.astype(F32).reshape(bd, -1, 1)
    out = pl.pallas_call(
        functools.partial(_fox_dec_kernel, gp=gp),
        grid_spec=pltpu.PrefetchScalarGridSpec(
            num_scalar_prefetch=1,
            grid=(bd, nj),
            in_specs=in_specs,
            out_specs=col(FOX_W),
            scratch_shapes=[pltpu.VMEM((FOX_H, 1), F32), pltpu.VMEM((FOX_H, 1), F32),
                            pltpu.VMEM((FOX_W, PAGE), F32), pltpu.VMEM((FOX_H, 1), F32),
                            pltpu.VMEM((FOX_W, PAGE), F32)]),
        out_shape=jax.ShapeDtypeStruct((bd, FOX_W, 1), F32),
        compiler_params=_cparams(("parallel", "arbitrary")),
        name="fox_decode",
    )(page_table, c3(qn_bf), c3(logf8), c3(kn), c3(v), c3(gate), u_mat,
      *([ck] * gp), *([cv] * gp), *([clf] * gp))
    return out.reshape(bd, FOX_W)


def _mla_dec_kernel(pt_ref, qa_ref, qr_ref, cnew_ref, pnew_ref, wkt_ref, *rest, gp):
    c_refs = rest[0:gp]
    p_refs = rest[gp:2 * gp]
    o_ref, m_ref, l_ref, acc_ref = rest[2 * gp:]
    j = pl.program_id(1)
    nj = pl.num_programs(1)

    @pl.when(j == 0)
    def _():
        m_ref[...] = jnp.full_like(m_ref, NEG_INF)
        l_ref[...] = jnp.zeros_like(l_ref)
        acc_ref[...] = jnp.zeros_like(acc_ref)

    qa = qa_ref[0]
    qr = qr_ref[0]

    def scores_t(cb, pt):
        n = cb.shape[0]
        kn = _dot_nt(wkt_ref[...], cb)
        ssq = jnp.sum((kn * kn).reshape(MLA_H, MLA_NOPE, n), axis=1)
        ssq = ssq + jnp.sum(pt * pt, axis=0, keepdims=True)
        raw = _dot_nt(qa, cb) + _dot(qr, pt.astype(BF16))
        return raw * lax.rsqrt(ssq * (1.0 / MLA_QK) + RMS_EPS)

    scores = []
    cbs = []
    for g in range(0, gp, 2):
        cb = jnp.concatenate([c_refs[g][0].astype(BF16), c_refs[g + 1][0].astype(BF16)], axis=0)
        pt = jnp.concatenate([p_refs[g][0], p_refs[g + 1][0]], axis=1)
        scores.append(scores_t(cb, pt))
        cbs.append(cb)
    m_prev = m_ref[...]
    m_new = m_prev
    for s in scores:
        m_new = jnp.maximum(m_new, jnp.max(s, axis=1, keepdims=True))
    alpha = jnp.exp(m_prev - m_new)
    l_new = alpha * l_ref[...]
    acc = alpha * acc_ref[...]
    for s, cb in zip(scores, cbs):
        p = jnp.exp(s - m_new)
        l_new = l_new + jnp.sum(p, axis=1, keepdims=True)
        acc = acc + _dot(p.astype(BF16), cb)
    m_ref[...] = m_new
    l_ref[...] = l_new
    acc_ref[...] = acc

    @pl.when(j == nj - 1)
    def _():
        cnew = cnew_ref[0]
        cblk = jnp.broadcast_to(cnew, (PAGE, MLA_KV_LORA)).astype(BF16)
        pblk = jnp.broadcast_to(pnew_ref[0], (MLA_ROPE, PAGE))
        s_new = scores_t(cblk, pblk)[:, 0:1]
        m_fin = jnp.maximum(m_new, s_new)
        a2 = jnp.exp(m_new - m_fin)
        p_new = jnp.exp(s_new - m_fin)
        l_fin = a2 * l_new + p_new
        num = a2 * acc + p_new * cnew
        o_ref[0] = num / l_fin


def _mla_decode(qa, qr, c_new, kpe_new, wkt_bf16, cache_c, cache_p, layer, page_table, *, gp=16):
    bd = qa.shape[0]
    n_pages = page_table.shape[1]
    n_pool = cache_c.shape[1]
    gp = min(gp, n_pages)
    nj = n_pages // gp
    cc = cache_c.reshape(-1, PAGE, MLA_KV_LORA)
    cp = jnp.transpose(cache_p, (0, 1, 3, 2)).reshape(-1, MLA_ROPE, PAGE)
    base = layer * n_pool
    full = lambda *s: pl.BlockSpec(s, lambda b, j, pt: (0,) * len(s))
    per_b = lambda *s: pl.BlockSpec((1,) + s, lambda b, j, pt: (b, 0, 0))

    def page_spec(r, c, g):
        return pl.BlockSpec((1, r, c), lambda b, j, pt: (base + pt[b, j * gp + g], 0, 0))

    in_specs = ([per_b(MLA_H, MLA_KV_LORA), per_b(MLA_H, MLA_ROPE), per_b(1, MLA_KV_LORA), per_b(MLA_ROPE, 1),
                 full(MLA_H * MLA_NOPE, MLA_KV_LORA)]
                + [page_spec(PAGE, MLA_KV_LORA, g) for g in range(gp)]
                + [page_spec(MLA_ROPE, PAGE, g) for g in range(gp)])
    lat = pl.pallas_call(
        functools.partial(_mla_dec_kernel, gp=gp),
        grid_spec=pltpu.PrefetchScalarGridSpec(
            num_scalar_prefetch=1,
            grid=(bd, nj),
            in_specs=in_specs,
            out_specs=per_b(MLA_H, MLA_KV_LORA),
            scratch_shapes=[pltpu.VMEM((MLA_H, 1), F32), pltpu.VMEM((MLA_H, 1), F32),
                            pltpu.VMEM((MLA_H, MLA_KV_LORA), F32)]),
        out_shape=jax.ShapeDtypeStruct((bd, MLA_H, MLA_KV_LORA), F32),
        compiler_params=_cparams(("parallel", "arbitrary")),
        name="mla_decode",
    )(page_table, qa, qr, c_new.reshape(bd, 1, MLA_KV_LORA), kpe_new.reshape(bd, MLA_ROPE, 1),
      wkt_bf16, *([cc] * gp), *([cp] * gp))
    return lat


def _latent_out_kernel(lat_ref, wv_ref, g_ref, o_ref):
    for h in range(MLA_H):
        o = _dot(lat_ref[:, h * MLA_KV_LORA:(h + 1) * MLA_KV_LORA].astype(BF16), wv_ref[h])
        sl = slice(h * MLA_V, (h + 1) * MLA_V)
        o_ref[:, sl] = o * _silu(g_ref[:, sl])


def _latent_out(lat, wv_bf16, gate):
    bd = lat.shape[0]
    return pl.pallas_call(
        _latent_out_kernel,
        out_shape=jax.ShapeDtypeStruct((bd, MLA_W), F32),
        compiler_params=pltpu.CompilerParams(vmem_limit_bytes=VMEM_LIMIT),
        name="latent_out",
    )(lat.reshape(bd, MLA_H * MLA_KV_LORA), wv_bf16, gate)


def _out_kernel(*refs, widths):
    n = len(widths)
    part_refs = refs[:n]
    w_ref, x_ref, o_ref = refs[n:]
    acc = x_ref[...]
    off = 0
    for p_ref, wd in zip(part_refs, widths):
        acc = acc + _dot(p_ref[...].astype(BF16), w_ref[off:off + wd, :])
        off += wd
    o_ref[...] = acc


def _out_proj(parts, w_bf16, x, *, tm=512):
    m, d = x.shape
    tm = min(tm, m)
    widths = tuple(p.shape[1] for p in parts)
    row = lambda w: pl.BlockSpec((tm, w), lambda i: (i, 0))
    return pl.pallas_call(
        functools.partial(_out_kernel, widths=widths),
        grid=(m // tm,),
        in_specs=[row(w) for w in widths] + [pl.BlockSpec(w_bf16.shape, lambda i: (0, 0)), row(d)],
        out_specs=row(d),
        out_shape=jax.ShapeDtypeStruct((m, d), F32),
        compiler_params=_cparams(("parallel",)),
        name="out_proj",
    )(*parts, w_bf16, x)


def _even_weights(w_in, b_f, q_gain, k_gain, w_g2):
    o = np.cumsum((0, FOX_W, FOX_W, FOX_W, FOX_H, FOX_W, GLA_KW, GLA_KW, GLA_W, GLA_RANK, GLA_W))
    seg = lambda i: w_in[:, o[i]:o[i + 1]]
    pad = jnp.zeros((w_in.shape[0], 128 - FOX_H - GLA_RANK), w_in.dtype)
    w = jnp.concatenate([seg(0), seg(1), seg(2), seg(4), seg(5), seg(6), seg(7), seg(9), seg(3), seg(8), pad],
                        axis=1).astype(BF16)
    bias128 = jnp.zeros((1, 128), F32).at[0, :FOX_H].set(b_f)
    qg = jnp.tile(q_gain, FOX_H).reshape(1, FOX_W)
    kg = jnp.tile(k_gain, FOX_H).reshape(1, FOX_W)
    w2pad = jnp.zeros((128, GLA_KW), F32).at[FOX_H:FOX_H + GLA_RANK].set(w_g2).astype(BF16)
    return w, bias128, qg, kg, w2pad


def _even_layer(xp, xs, e, cache_k, cache_v, cache_lf, state, page_table, norm, w_in, b_f, q_gain, k_gain,
                w_g2, b_g, out_gain, w_out):
    bp, tp, d = xp.shape
    bd = xs.shape[0]
    w, bias128, qg, kg, w2pad = _even_weights(w_in, b_f, q_gain, k_gain, w_g2)
    bg = b_g.reshape(1, GLA_KW)
    gain = out_gain.reshape(1, GLA_DV)
    w_out_bf = w_out.astype(BF16)

    x2 = xp.reshape(bp * tp, d)
    qa, ka, kn, v, vt, rest, misc = _even_in(x2, norm, w, bias128, qg, kg, tp, FOX_SCALE * LOG2E)
    r3 = lambda a: a.reshape(bp, tp, a.shape[-1])
    rest3 = r3(rest)
    fox = _flash(r3(qa), r3(ka), vt, rest3, 0, FOX_H, FOX_D, "flash_fox")
    gla, sfin = _gla_prompt(rest3, r3(misc), w2pad, bg, gain)
    xp_new = _out_proj([fox.reshape(bp * tp, FOX_W), gla.reshape(bp * tp, GLA_W)], w_out_bf, x2).reshape(bp, tp, d)
    outs_p = (kn.reshape(bp, tp, FOX_H, FOX_D), v.reshape(bp, tp, FOX_H, FOX_D),
              r3(misc)[:, :, :FOX_H], sfin.transpose(0, 1, 3, 2))

    xs2 = xs.reshape(bd, d)
    qa, ka, kn, v, vt, rest, misc = _even_in(xs2, norm, w, bias128, qg, kg, 1, FOX_SCALE)
    qn = qa.reshape(bd, FOX_H, HP)[:, :, :FOX_D].reshape(bd, FOX_W)
    fox = _fox_decode(qn, kn, v, misc[:, :FOX_H], rest[:, :FOX_W], cache_k, cache_v, cache_lf, e, page_table)
    gla, snew = _gla_step(rest, misc, state, w_g2, b_g, gain)
    xs_new = _out_proj([fox, gla], w_out_bf, xs2).reshape(bd, 1, d)
    outs_s = (kn.reshape(bd, 1, FOX_H, FOX_D), v.reshape(bd, 1, FOX_H, FOX_D),
              misc[:, :FOX_H].reshape(bd, 1, FOX_H), snew)
    return xp_new, xs_new, outs_p, outs_s


def _odd_weights(w_in, w_q_b, w_kv_b, q_gain, k_gain):
    rows = w_in.shape[0]
    o = np.cumsum((0, MLA_Q_LORA, MLA_KV_LORA, MLA_ROPE, MLA_W))
    z = lambda n: jnp.zeros((rows, n), w_in.dtype)
    w = jnp.concatenate([w_in[:, o[0]:o[1]], w_in[:, o[1]:o[2]], z(ROPE_LO), w_in[:, o[2]:o[3]],
                         z(HP - ROPE_LO - MLA_ROPE), w_in[:, o[3]:o[4]]], axis=1).astype(BF16)
    wq = jnp.pad(w_q_b.reshape(MLA_Q_LORA, MLA_H, MLA_QK), ((0, 0), (0, 0), (0, HP - MLA_QK)))
    wq = wq.reshape(MLA_Q_LORA, MLA_H * HP).astype(BF16)
    wkv = w_kv_b.reshape(MLA_KV_LORA, MLA_H, MLA_NOPE + MLA_V)
    wk = wkv[:, :, :MLA_NOPE]
    wk_pad = jnp.pad(wk, ((0, 0), (0, 0), (0, HP - MLA_NOPE))).reshape(MLA_KV_LORA, MLA_H * HP).astype(BF16)
    wv = wkv[:, :, MLA_NOPE:]
    padg = lambda g: jnp.pad(g, (0, HP - MLA_QK)).reshape(1, HP)
    return w, wq, wk, wk_pad, wv, padg(q_gain), padg(k_gain)


def _rope_tiles(pos):
    inv = ROPE_THETA ** (-jnp.arange(ROPE_HALF, dtype=F32) / ROPE_HALF)
    ang = pos.astype(F32)[:, None] * inv[None, :]
    cos, sin = jnp.cos(ang), jnp.sin(ang)
    n = pos.shape[0]
    one = jnp.ones((n, ROPE_LO), F32)
    z = lambda w: jnp.zeros((n, w), F32)
    tail = HP - ROPE_LO - MLA_ROPE
    cos_t = jnp.concatenate([one, cos, cos, z(tail)], axis=1)
    sa_t = jnp.concatenate([z(ROPE_LO + ROPE_HALF), sin, z(tail)], axis=1)
    sb_t = jnp.concatenate([z(ROPE_LO), -sin, z(ROPE_HALF + tail)], axis=1)
    return cos_t, sa_t, sb_t


def _odd_layer(xp, xs, o, cache_c, cache_p, page_table, norm, w_in, q_a_gain, w_q_b, kv_a_gain, w_kv_b,
               q_gain, k_gain, w_out):
    bp, tp, d = xp.shape
    bd = xs.shape[0]
    past = page_table.shape[1] * PAGE
    w, wq, wk, wk_pad, wv, qg, kg = _odd_weights(w_in, w_q_b, w_kv_b, q_gain, k_gain)
    wv_flat = wv.reshape(MLA_KV_LORA, MLA_W).astype(BF16)
    w_out_bf = w_out.astype(BF16)

    x2 = xp.reshape(bp * tp, d)
    q, k, vt, c, kpe, gate = _odd_in(x2, norm, w, q_a_gain, wq, kv_a_gain, wk_pad, wv_flat, qg, kg,
                                     _rope_tiles(jnp.arange(tp)), MLA_SCALE * LOG2E)
    r3 = lambda a: a.reshape(bp, tp, a.shape[-1])
    att = _flash(r3(q), r3(k), vt, r3(gate), 0, MLA_H, MLA_V, "flash_mla")
    xp_new = _out_proj([att.reshape(bp * tp, MLA_W)], w_out_bf, x2).reshape(bp, tp, d)
    outs_p = (c.reshape(bp, tp, MLA_KV_LORA), kpe.reshape(bp, tp, MLA_ROPE))

    xs2 = xs.reshape(bd, d)
    q, _, _, c, kpe, gate = _odd_in(xs2, norm, w, q_a_gain, wq, kv_a_gain, wk_pad, wv_flat, qg, kg,
                                    _rope_tiles(jnp.full((bd,), past, jnp.int32)), MLA_SCALE)
    qk = q.astype(F32).reshape(bd, MLA_H, HP)[:, :, :MLA_QK] * k_gain
    qa = jnp.einsum('bhd,rhd->bhr', qk[:, :, :MLA_NOPE], wk).astype(BF16)
    qr = qk[:, :, MLA_NOPE:].astype(BF16)
    wkt = wk.reshape(MLA_KV_LORA, MLA_H * MLA_NOPE).T.astype(BF16)
    lat = _mla_decode(qa, qr, c, kpe, wkt, cache_c, cache_p, o, page_table)
    att = _latent_out(lat, wv.transpose(1, 0, 2).astype(BF16), gate)
    xs_new = _out_proj([att], w_out_bf, xs2).reshape(bd, 1, d)
    outs_s = (c.reshape(bd, 1, MLA_KV_LORA), kpe.reshape(bd, 1, MLA_ROPE))
    return xp_new, xs_new, outs_p, outs_s


def kernel(x_prompt, x_sample, cache_fox_k, cache_fox_v, cache_fox_logf, cache_mla_ckv, cache_mla_kpe,
           state_gla, page_table, norm_even, w_in_even, b_fox_f, fox_q_gain, fox_k_gain, gla_w_gate2,
           gla_b_gate, gla_out_gain, w_out_even, norm_odd, w_in_odd, mla_q_a_gain, w_q_b, mla_kv_a_gain,
           w_kv_b, mla_q_gain, mla_k_gain, w_out_odd):
    depth = norm_even.shape[0] + norm_odd.shape[0]
    xp, xs = x_prompt, x_sample
    ev_p, ev_s, od_p, od_s = [], [], [], []
    for layer in range(depth):
        i = layer // 2
        if layer % 2 == 0:
            xp, xs, op, os_ = _even_layer(
                xp, xs, i, cache_fox_k, cache_fox_v, cache_fox_logf, state_gla[i], page_table,
                norm_even[i], w_in_even[i], b_fox_f[i], fox_q_gain[i], fox_k_gain[i], gla_w_gate2[i],
                gla_b_gate[i], gla_out_gain[i], w_out_even[i])
            ev_p.append(op)
            ev_s.append(os_)
        else:
            xp, xs, op, os_ = _odd_layer(
                xp, xs, i, cache_mla_ckv, cache_mla_kpe, page_table, norm_odd[i], w_in_odd[i],
                mla_q_a_gain[i], w_q_b[i], mla_kv_a_gain[i], w_kv_b[i], mla_q_gain[i], mla_k_gain[i],
                w_out_odd[i])
            od_p.append(op)
            od_s.append(os_)
    st = lambda lst, k: jnp.stack([t[k] for t in lst])
    return (xp, xs,
            st(ev_p, 0), st(ev_p, 1), st(ev_p, 2), st(od_p, 0), st(od_p, 1), st(ev_p, 3),
            st(ev_s, 0), st(ev_s, 1), st(ev_s, 2), st(od_s, 0), st(od_s, 1), st(ev_s, 3))
```

```python
import functools

import numpy as np
import jax
import jax.numpy as jnp
from jax import lax
from jax.experimental import pallas as pl
from jax.experimental.pallas import tpu as pltpu

F32 = jnp.float32
BF16 = jnp.bfloat16

D_MODEL = 1024
PAGE = 128
RMS_EPS = 1e-6
NEG_INF = -1e30
FOX_H = 8
FOX_D = 64
FOX_W = FOX_H * FOX_D
FOX_SCALE = FOX_D ** -0.5
GLA_H = 4
GLA_DK = 64
GLA_DV = 128
GLA_KW = GLA_H * GLA_DK
GLA_W = GLA_H * GLA_DV
GLA_RANK = 16
GLA_GATE_NORM = 16.0
GLA_CHUNK = 64
MLA_H = 16
MLA_Q_LORA = 256
MLA_KV_LORA = 128
MLA_NOPE = 64
MLA_ROPE = 32
MLA_QK = MLA_NOPE + MLA_ROPE
MLA_V = 64
MLA_W = MLA_H * MLA_V
MLA_SCALE = MLA_QK ** -0.5
HP = 128
ROPE_THETA = 10000.0
LOG2E = 1.4426950408889634

VMEM_LIMIT = 56 * 1024 * 1024

E_Q, E_K, E_V, E_REST, E_MISC, E_END = 0, 512, 1024, 1536, 3584, 3712


def _cparams(sem):
    return pltpu.CompilerParams(dimension_semantics=sem, vmem_limit_bytes=VMEM_LIMIT)


def _log_sigmoid(x):
    return jnp.minimum(x, 0.0) - jnp.log1p(jnp.exp(-jnp.abs(x)))


def _silu(x):
    return x / (1.0 + jnp.exp(-x))


def _split_bf16(x):
    hi = x.astype(BF16)
    lo = (x - hi.astype(F32)).astype(BF16)
    return hi, lo


def _dot(a, b):
    return jnp.dot(a, b, preferred_element_type=F32)


def _dot_nt(a, b):
    return lax.dot_general(a, b, (((1,), (1,)), ((), ())), preferred_element_type=F32)


def _split3_bf16(x):
    hi = x.astype(BF16)
    r = x - hi.astype(F32)
    mid = r.astype(BF16)
    lo = (r - mid.astype(F32)).astype(BF16)
    return hi, mid, lo


def _rmsnorm(x, gain):
    return x * lax.rsqrt(jnp.mean(x * x, axis=-1, keepdims=True) + RMS_EPS) * gain


O_QA, O_KVA, O_KPE, O_G, O_END = 0, 256, 384, 512, 1536
ROPE_LO = MLA_NOPE
ROPE_HALF = MLA_ROPE // 2


def _odd_in_kernel(x_ref, g_ref, w_ref, qag_ref, wq_ref, kvg_ref, wk_ref, wv_ref, qg_ref, kg_ref, ones_ref,
                   cos_ref, sa_ref, sb_ref, q_ref, k_ref, vt_ref, c_ref, kpe_ref, gate_ref, *, q_scale):
    xn = _rmsnorm(x_ref[...], g_ref[...]).astype(BF16)

    def mm(lo, hi):
        return _dot(xn, w_ref[:, lo:hi])

    cos, sa, sb = cos_ref[...], sa_ref[...], sb_ref[...]

    def rope(t):
        return (t * cos + pltpu.roll(t, ROPE_HALF, 1) * sa + pltpu.roll(t, HP - ROPE_HALF, 1) * sb)

    def headnorm(t, gain):
        ss = _dot((t * t).astype(BF16), ones_ref[...])
        return t * lax.rsqrt(ss * (1.0 / MLA_QK) + RMS_EPS) * gain

    gate_ref[...] = mm(O_G, O_END)
    c = _rmsnorm(mm(O_KVA, O_KPE), kvg_ref[...])
    c_ref[...] = c
    kpe_t = rope(mm(O_KPE, O_G))
    kpe_ref[...] = kpe_t[:, ROPE_LO:ROPE_LO + MLA_ROPE]
    qa = _rmsnorm(mm(O_QA, O_KVA), qag_ref[...]).astype(BF16)
    cb = c.astype(BF16)
    for h in range(MLA_H):
        sl = slice(h * HP, (h + 1) * HP)
        qt = rope(_dot(qa, wq_ref[:, sl]))
        q_ref[:, sl] = (headnorm(qt, qg_ref[...]) * q_scale).astype(BF16)
        kt = _dot(cb, wk_ref[:, sl]) + kpe_t
        k_ref[:, sl] = headnorm(kt, kg_ref[...]).astype(BF16)
    vt_ref[...] = _dot(cb, wv_ref[...]).T.astype(BF16)


def _odd_in(x, gain, w, qag, wq, kvg, wk, wv, qg, kg, tables, q_scale, *, tm=256):
    m, d = x.shape
    tm = min(tm, m)
    cos_t, sa_t, sb_t = tables
    n_tab = cos_t.shape[0] // tm
    row = lambda wd: pl.BlockSpec((tm, wd), lambda i: (i, 0))
    full = lambda a, b: pl.BlockSpec((a, b), lambda i: (0, 0))
    tab = pl.BlockSpec((tm, HP), lambda i: (i % n_tab, 0))
    ones = jnp.ones((HP, HP), BF16)
    return pl.pallas_call(
        functools.partial(_odd_in_kernel, q_scale=q_scale),
        grid=(m // tm,),
        in_specs=[row(d), full(1, d), full(d, O_END), full(1, MLA_Q_LORA), full(MLA_Q_LORA, MLA_H * HP),
                  full(1, MLA_KV_LORA), full(MLA_KV_LORA, MLA_H * HP), full(MLA_KV_LORA, MLA_W),
                  full(1, HP), full(1, HP), full(HP, HP), tab, tab, tab],
        out_specs=[row(MLA_H * HP), row(MLA_H * HP), pl.BlockSpec((MLA_W, tm), lambda i: (0, i)),
                   row(MLA_KV_LORA), row(MLA_ROPE), row(MLA_W)],
        out_shape=[jax.ShapeDtypeStruct((m, MLA_H * HP), BF16),
                   jax.ShapeDtypeStruct((m, MLA_H * HP), BF16),
                   jax.ShapeDtypeStruct((MLA_W, m), BF16),
                   jax.ShapeDtypeStruct((m, MLA_KV_LORA), F32),
                   jax.ShapeDtypeStruct((m, MLA_ROPE), F32),
                   jax.ShapeDtypeStruct((m, MLA_W), F32)],
        compiler_params=_cparams(("parallel",)),
        name="odd_in_proj",
    )(x, gain.reshape(1, d), w, qag.reshape(1, -1), wq, kvg.reshape(1, -1), wk, wv, qg, kg, ones,
      cos_t, sa_t, sb_t)


AUG_Q = FOX_D
AUG_K = FOX_D + 3


def _even_in_kernel(x_ref, g_ref, w_ref, bias_ref, qg_ref, kg_ref, ones_ref, tri_ref, place_ref, pc_ref, aug_ref,
                    qa_ref, ka_ref, kn_ref, v_ref, vt_ref, rest_ref, misc_ref,
                    carry_ref, *, tiles_per_seq, do_cum, q_scale):
    xn = _rmsnorm(x_ref[...], g_ref[...]).astype(BF16)

    def mm(lo, hi):
        return _dot(xn, w_ref[:, lo:hi])

    def headnorm(z, gain):
        ss = _dot((z * z).astype(BF16), ones_ref[...])
        return z * lax.rsqrt(ss * (1.0 / FOX_D) + RMS_EPS) * gain

    q = headnorm(mm(E_Q, E_K), qg_ref[...]) * q_scale
    k = headnorm(mm(E_K, E_V), kg_ref[...])
    v = mm(E_V, E_REST)
    vt = v.T
    vt_ref[...] = vt.astype(BF16)
    if do_cum:
        kn_ref[0] = k.T
        v_ref[0] = vt
    else:
        kn_ref[...] = k
        v_ref[...] = v
    rest_ref[...] = mm(E_REST, E_MISC)
    mz = mm(E_MISC, E_END) + bias_ref[...]
    lane = lax.broadcasted_iota(jnp.int32, mz.shape, 1)
    logf = jnp.where(lane < FOX_H, _log_sigmoid(mz), mz)
    misc_ref[...] = logf
    qa = _dot(q.astype(BF16), place_ref[...])
    ka = _dot(k.astype(BF16), place_ref[...])
    if do_cum:
        @pl.when(pl.program_id(0) % tiles_per_seq == 0)
        def _():
            carry_ref[...] = jnp.zeros_like(carry_ref)

        lf = jnp.where(lane < FOX_H, logf, 0.0)
        hi, lo = _split_bf16(lf)
        c = _dot(tri_ref[...], hi) + _dot(tri_ref[...], lo) + carry_ref[...]
        carry_ref[...] = c[c.shape[0] - 1:, :]
        pieces = _split3_bf16(c * LOG2E)
        for p in range(3):
            qa = qa + _dot(pieces[p], pc_ref[p])
            ka = ka + _dot(pieces[p], pc_ref[3 + p])
        qa = qa + aug_ref[0:1, :]
        ka = ka + aug_ref[1:2, :]
    qa_ref[...] = qa.astype(BF16)
    ka_ref[...] = ka.astype(BF16)


def _even_in(x, gain, w_bf16, bias128, qg, kg, seq_len, q_scale, *, tm=256):
    m, k = x.shape
    tm = min(tm, m, seq_len) if seq_len > 1 else min(tm, m)
    do_cum = seq_len > 1
    tiles_per_seq = max(seq_len // tm, 1)
    ones_bd = jnp.asarray(np.kron(np.eye(FOX_H), np.ones((FOX_D, FOX_D))), BF16)
    tri = jnp.asarray(np.tril(np.ones((tm, tm))), BF16)
    place = np.zeros((FOX_H, FOX_D, FOX_H, HP), np.float32)
    pc = np.zeros((6, 128, FOX_H, HP), np.float32)
    aug = np.zeros((2, FOX_H, HP), np.float32)
    for h in range(FOX_H):
        place[h, np.arange(FOX_D), h, np.arange(FOX_D)] = 1.0
        for p in range(3):
            pc[p, h, h, AUG_Q + p] = 1.0
            pc[3 + p, h, h, AUG_K + p] = -1.0
        aug[0, h, AUG_K:AUG_K + 3] = 1.0
        aug[1, h, AUG_Q:AUG_Q + 3] = 1.0
    place = jnp.asarray(place.reshape(FOX_W, FOX_H * HP), BF16)
    pc = jnp.asarray(pc.reshape(6, 128, FOX_H * HP), BF16)
    aug = jnp.asarray(aug.reshape(2, FOX_H * HP), F32)
    row = lambda w: pl.BlockSpec((tm, w), lambda i: (i, 0))
    full = lambda *s: pl.BlockSpec(s, lambda i: (0,) * len(s))
    if do_cum:
        kv_spec = pl.BlockSpec((1, FOX_W, tm), lambda i: (i // tiles_per_seq, 0, i % tiles_per_seq))
        kv_shape = jax.ShapeDtypeStruct((m // seq_len, FOX_W, seq_len), F32)
    else:
        kv_spec = row(FOX_W)
        kv_shape = jax.ShapeDtypeStruct((m, FOX_W), F32)
    outs = pl.pallas_call(
        functools.partial(_even_in_kernel, tiles_per_seq=tiles_per_seq, do_cum=do_cum, q_scale=q_scale),
        grid=(m // tm,),
        in_specs=[row(k), full(1, k), full(k, E_END), full(1, 128), full(1, FOX_W), full(1, FOX_W),
                  full(FOX_W, FOX_W), full(tm, tm), full(FOX_W, FOX_H * HP), full(6, 128, FOX_H * HP),
                  full(2, FOX_H * HP)],
        out_specs=[row(FOX_H * HP), row(FOX_H * HP), kv_spec, kv_spec,
                   pl.BlockSpec((FOX_W, tm), lambda i: (0, i)),
                   row(E_MISC - E_REST), row(128)],
        out_shape=[jax.ShapeDtypeStruct((m, FOX_H * HP), BF16),
                   jax.ShapeDtypeStruct((m, FOX_H * HP), BF16),
                   kv_shape, kv_shape,
                   jax.ShapeDtypeStruct((FOX_W, m), BF16),
                   jax.ShapeDtypeStruct((m, E_MISC - E_REST), F32),
                   jax.ShapeDtypeStruct((m, 128), F32)],
        scratch_shapes=[pltpu.VMEM((1, 128), F32)],
        compiler_params=_cparams(("arbitrary",)),
        name="even_in_proj",
    )(x, gain.reshape(1, k), w_bf16, bias128, qg, kg, ones_bd, tri, place, pc, aug)
    return outs


ONES_ROWS = 16


def _flash_kernel(q_ref, k_ref, vt_ref, g_ref, o_ref, m_ref, acc_ref, *, n_heads, dv):
    i = pl.program_id(1)
    j = pl.program_id(2)

    @pl.when(j == 0)
    def _():
        m_ref[...] = jnp.full_like(m_ref, NEG_INF)
        acc_ref[...] = jnp.zeros_like(acc_ref)

    def step(masked):
        tq = q_ref.shape[1]
        tk = k_ref.shape[1]
        ones = jnp.ones((ONES_ROWS, tk), BF16)
        if masked:
            r = lax.broadcasted_iota(jnp.int32, (tk, tq), 0)
            c = lax.broadcasted_iota(jnp.int32, (tk, tq), 1)
            keep = c >= r
        for h in range(n_heads):
            q = q_ref[0, :, h * HP:(h + 1) * HP]
            k = k_ref[0, :, h * HP:(h + 1) * HP]
            s = _dot_nt(k, q)
            if masked:
                s = jnp.where(keep, s, NEG_INF)
            m_prev = m_ref[h]
            m_new = jnp.maximum(m_prev, jnp.max(s, axis=0, keepdims=True))
            alpha = jnp.exp2(m_prev - m_new)
            p = jnp.exp2(s - m_new)
            va = jnp.concatenate([vt_ref[h * dv:(h + 1) * dv, :], ones], axis=0)
            acc_ref[h] = alpha * acc_ref[h] + _dot(va, p.astype(BF16))
            m_ref[h] = m_new

    @pl.when(j < i)
    def _():
        step(False)

    @pl.when(j == i)
    def _():
        step(True)
        for h in range(0, n_heads, 2):
            sl = slice(h * dv, (h + 2) * dv)
            a0, a1 = acc_ref[h], acc_ref[h + 1]
            o = jnp.concatenate([a0[:dv] / a0[dv:dv + 1], a1[:dv] / a1[dv:dv + 1]], axis=0)
            o_ref[0, :, sl] = (o.T * _silu(g_ref[0, :, sl])).astype(o_ref.dtype)


def _flash(q, k, vt, gate, gate_block, n_heads, dv, name, *, t_blk=512):
    b, t, _ = q.shape
    tb = min(t_blk, t)
    n = t // tb
    qmap = lambda bi, i, j: (bi, i, 0)
    kmap = lambda bi, i, j: (bi, jnp.minimum(i, j), 0)
    in_specs = [pl.BlockSpec((1, tb, n_heads * HP), qmap),
                pl.BlockSpec((1, tb, n_heads * HP), kmap),
                pl.BlockSpec((n_heads * dv, tb), lambda bi, i, j: (0, bi * n + jnp.minimum(i, j))),
                pl.BlockSpec((1, tb, n_heads * dv), lambda bi, i, j: (bi, i, gate_block))]
    return pl.pallas_call(
        functools.partial(_flash_kernel, n_heads=n_heads, dv=dv),
        grid=(b, n, n),
        in_specs=in_specs,
        out_specs=pl.BlockSpec((1, tb, n_heads * dv), qmap),
        out_shape=jax.ShapeDtypeStruct((b, t, n_heads * dv), BF16),
        scratch_shapes=[pltpu.VMEM((n_heads, 1, tb), F32),
                        pltpu.VMEM((n_heads, dv + ONES_ROWS, tb), F32)],
        compiler_params=_cparams(("parallel", "parallel", "arbitrary")),
        name=name,
    )(q, k, vt, gate)


PAIR_BLK = 16


def _gla_gate(misc, w2_ref, bg_ref):
    pre = _dot(misc.astype(BF16), w2_ref[...]) + bg_ref[...]
    return _log_sigmoid(pre) * (1.0 / GLA_GATE_NORM)


def _gla_kernel(q_ref, k_ref, v_ref, gg_ref, misc_ref, w2_ref, bg_ref, gain_ref, hsel_ref, tri_ref,
                o_ref, sfin_ref, st_ref, cum_s, k_s, w_s):
    c = pl.program_id(1)
    n_c = pl.num_programs(1)
    ch = GLA_CHUNK

    @pl.when(c == 0)
    def _():
        st_ref[...] = jnp.zeros_like(st_ref)

    q = q_ref[0] * (GLA_DK ** -0.5)
    k = k_ref[0]
    v = v_ref[0]
    la = _gla_gate(misc_ref[0], w2_ref, bg_ref)
    hi, lo = _split_bf16(la)
    cum = _dot(tri_ref[...], hi) + _dot(tri_ref[...], lo)
    cum_s[...] = cum
    k_s[...] = k

    @pl.when(c == 0)
    def _():
        w_s[...] = jnp.zeros_like(w_s)

    for jb in range(ch // PAIR_BLK):
        r0 = jb * PAIR_BLK
        cum_r = cum[r0:, :]
        q_r = q[r0:, :]

        def body(j, carry, r0=r0, cum_r=cum_r, q_r=q_r):
            rowc = cum_s[pl.ds(j, 1), :]
            rowk = k_s[pl.ds(j, 1), :]
            w = (jnp.exp(jnp.minimum(cum_r - rowc, 0.0)) * (q_r * rowk)).astype(BF16)
            off = pl.multiple_of(j * 128, 128)
            w_s[0, r0:, pl.ds(off, 128)] = w[:, 0:128]
            w_s[1, r0:, pl.ds(off, 128)] = w[:, 128:256]
            return carry

        lax.fori_loop(r0, r0 + PAIR_BLK, body, 0, unroll=8)
    r = lax.broadcasted_iota(jnp.int32, (ch, ch), 0)
    cc = lax.broadcasted_iota(jnp.int32, (ch, ch), 1)
    causal = r >= cc
    for lt in range(2):
        a2 = _dot(w_s[lt], hsel_ref[...])
        for hh in range(2):
            h = 2 * lt + hh
            sk = slice(h * GLA_DK, (h + 1) * GLA_DK)
            sv = slice(h * GLA_DV, (h + 1) * GLA_DV)
            a = jnp.where(causal, a2[:, hh * ch:(hh + 1) * ch], 0.0)
            vh = v[:, sv]
            vb = vh.astype(BF16)
            cum_h = cum[:, sk]
            last = cum_h[ch - 1:, :]
            st = st_ref[h]
            qe = (q[:, sk] * jnp.exp(cum_h)).astype(BF16)
            o = _dot_nt(qe, st.astype(BF16)) + _dot(a.astype(BF16), vb)
            kd = (k[:, sk] * jnp.exp(last - cum_h)).astype(BF16)
            st_ref[h] = st * jnp.exp(last) + _dot(vh.T.astype(BF16), kd)
            ms = jnp.mean(o * o, axis=-1, keepdims=True)
            y = o * lax.rsqrt(ms + RMS_EPS) * gain_ref[...] * _silu(gg_ref[0, :, sv])
            o_ref[0, :, sv] = y.astype(o_ref.dtype)

    @pl.when(c == n_c - 1)
    def _():
        sfin_ref[0] = st_ref[...]


def _gla_prompt(rest, misc, w2pad, bg, gain):
    b, t, _ = rest.shape
    ch = GLA_CHUNK
    n = t // ch
    hsel = np.zeros((ch, 2, GLA_DK, 2, ch), np.float32)
    for hh in range(2):
        for j in range(ch):
            hsel[j, hh, :, hh, j] = 1.0
    hsel = jnp.asarray(hsel.reshape(ch * 128, 128), BF16)
    tri = jnp.asarray(np.tril(np.ones((ch, ch))), BF16)
    full = lambda *s: pl.BlockSpec(s, lambda bi, ci: (0,) * len(s))
    o, sfin = pl.pallas_call(
        _gla_kernel,
        grid=(b, n),
        in_specs=[pl.BlockSpec((1, ch, GLA_KW), lambda bi, ci: (bi, ci, 2)),
                  pl.BlockSpec((1, ch, GLA_KW), lambda bi, ci: (bi, ci, 3)),
                  pl.BlockSpec((1, ch, GLA_W), lambda bi, ci: (bi, ci, 2)),
                  pl.BlockSpec((1, ch, GLA_W), lambda bi, ci: (bi, ci, 3)),
                  pl.BlockSpec((1, ch, 128), lambda bi, ci: (bi, ci, 0)),
                  full(128, GLA_KW), full(1, GLA_KW), full(1, GLA_DV), full(ch * 128, 128), full(ch, ch)],
        out_specs=[pl.BlockSpec((1, ch, GLA_W), lambda bi, ci: (bi, ci, 0)),
                   pl.BlockSpec((1, GLA_H, GLA_DV, GLA_DK), lambda bi, ci: (bi, 0, 0, 0))],
        out_shape=[jax.ShapeDtypeStruct((b, t, GLA_W), BF16),
                   jax.ShapeDtypeStruct((b, GLA_H, GLA_DV, GLA_DK), F32)],
        scratch_shapes=[pltpu.VMEM((GLA_H, GLA_DV, GLA_DK), F32),
                        pltpu.VMEM((ch, GLA_KW), F32),
                        pltpu.VMEM((ch, GLA_KW), F32),
                        pltpu.VMEM((2, ch, ch * 128), BF16)],
        compiler_params=_cparams(("parallel", "arbitrary")),
        name="gla_prompt",
    )(rest, rest, rest, rest, misc, w2pad, bg, gain, hsel, tri)
    return o, sfin


def _gla_step_kernel(qT_ref, kT_ref, gaT_ref, v_ref, gg_ref, s_ref, w2T_ref, bgT_ref, gain_ref,
                     o_ref, snew_ref, *, tb):
    laT = _log_sigmoid(_dot(w2T_ref[...], gaT_ref[0].astype(BF16)) + bgT_ref[...]) * (1.0 / GLA_GATE_NORM)
    eT = jnp.exp(laT)
    for bb in range(tb):
        for h in range(GLA_H):
            sk = slice(h * GLA_DK, (h + 1) * GLA_DK)
            sv = slice(h * GLA_DV, (h + 1) * GLA_DV)
            s_new = s_ref[bb, h] * eT[sk, bb:bb + 1] + kT_ref[0, sk, bb:bb + 1] * v_ref[bb, :, sv]
            snew_ref[bb, h] = s_new
            qcol = qT_ref[0, sk, bb:bb + 1] * (GLA_DK ** -0.5)
            o = jnp.sum(qcol * s_new, axis=0, keepdims=True)
            ms = jnp.mean(o * o, axis=-1, keepdims=True)
            y = o * lax.rsqrt(ms + RMS_EPS) * gain_ref[...] * _silu(gg_ref[bb, :, sv])
            o_ref[bb, :, sv] = y


def _gla_step(rest, misc, state, w2, bg, gain, *, tb=8):
    bd = rest.shape[0]
    tb = min(tb, bd)
    g = bd // tb
    tr = lambda a: a.reshape(g, tb, a.shape[-1]).transpose(0, 2, 1)
    qT = tr(rest[:, 512:768])
    kT = tr(rest[:, 768:1024])
    gaT = tr(misc[:, 8:8 + GLA_RANK])
    v3 = rest[:, 1024:1536].reshape(bd, 1, GLA_W)
    gg3 = rest[:, 1536:2048].reshape(bd, 1, GLA_W)
    blk = lambda *s: pl.BlockSpec(s, lambda i: (i,) + (0,) * (len(s) - 1))
    full = lambda *s: pl.BlockSpec(s, lambda i: (0,) * len(s))
    o, snew = pl.pallas_call(
        functools.partial(_gla_step_kernel, tb=tb),
        grid=(g,),
        in_specs=[blk(1, GLA_KW, tb), blk(1, GLA_KW, tb), blk(1, GLA_RANK, tb),
                  blk(tb, 1, GLA_W), blk(tb, 1, GLA_W), blk(tb, GLA_H, GLA_DK, GLA_DV),
                  full(GLA_KW, GLA_RANK), full(GLA_KW, 1), full(1, GLA_DV)],
        out_specs=[blk(tb, 1, GLA_W), blk(tb, GLA_H, GLA_DK, GLA_DV)],
        out_shape=[jax.ShapeDtypeStruct((bd, 1, GLA_W), F32),
                   jax.ShapeDtypeStruct((bd, GLA_H, GLA_DK, GLA_DV), F32)],
        compiler_params=_cparams(("parallel",)),
        name="gla_step",
    )(qT, kT, gaT, v3, gg3, state, w2.T.astype(BF16), bg.reshape(GLA_KW, 1), gain)
    return o.reshape(bd, GLA_W), snew


def _head_rows(a, n_heads, width):
    return jnp.concatenate([jnp.broadcast_to(a[h:h + 1, :], (width, 1)) for h in range(n_heads)], axis=0)


def _fox_dec_kernel(pt_ref, qbd_ref, cnew_ref, knew_ref, vnew_ref, g_ref, u_ref, *rest, gp):
    k_refs = rest[0:gp]
    v_refs = rest[gp:2 * gp]
    lf_refs = rest[2 * gp:3 * gp]
    o_ref, m_ref, l_ref, acc_ref, sfx_ref = rest[3 * gp:]
    j = pl.program_id(1)
    nj = pl.num_programs(1)

    @pl.when(j == 0)
    def _():
        m_ref[...] = jnp.full_like(m_ref, NEG_INF)
        l_ref[...] = jnp.zeros_like(l_ref)
        acc_ref[...] = jnp.zeros_like(acc_ref)
        sfx_ref[...] = jnp.zeros_like(sfx_ref)

    qbd = qbd_ref[0]
    cnew = cnew_ref[0]
    carry = sfx_ref[...]
    scores = []
    for g in range(gp):
        lf = lf_refs[g][0]
        hi, lo = _split_bf16(lf)
        sfx = _dot(hi, u_ref[...]) + _dot(lo, u_ref[...]) + carry
        carry = carry + jnp.sum(lf, axis=1, keepdims=True)
        s = _dot(qbd, k_refs[g][0].astype(BF16)) + sfx + cnew
        scores.append(s)
    sfx_ref[...] = carry
    m_prev = m_ref[...]
    m_new = m_prev
    for s in scores:
        m_new = jnp.maximum(m_new, jnp.max(s, axis=1, keepdims=True))
    alpha = jnp.exp(m_prev - m_new)
    ps = [jnp.exp(s - m_new) for s in scores]
    l_new = alpha * l_ref[...]
    for p in ps:
        l_new = l_new + jnp.sum(p, axis=1, keepdims=True)
    for h in range(FOX_H):
        rows = slice(h * FOX_D, (h + 1) * FOX_D)
        a = acc_ref[rows, :] * alpha[h:h + 1, :]
        for g in range(gp):
            a = a + ps[g][h:h + 1, :] * v_refs[g][0, rows, :]
        acc_ref[rows, :] = a
    m_ref[...] = m_new
    l_ref[...] = l_new

    @pl.when(j == nj - 1)
    def _():
        s_new = jnp.sum(qbd.astype(F32) * knew_ref[0], axis=1, keepdims=True)
        m_fin = jnp.maximum(m_new, s_new)
        a2 = jnp.exp(m_new - m_fin)
        p_new = jnp.exp(s_new - m_fin)
        l_fin = a2 * l_new + p_new
        tot = jnp.sum(acc_ref[...], axis=1, keepdims=True)
        num = _head_rows(a2, FOX_H, FOX_D) * tot + _head_rows(p_new, FOX_H, FOX_D) * vnew_ref[0]
        o = num / _head_rows(l_fin, FOX_H, FOX_D)
        o_ref[0] = o * _silu(g_ref[0])


def _fox_decode(qn_bf, kn, v, logf8, gate, cache_k, cache_v, cache_lf, layer, page_table, *, gp=16):
    bd = qn_bf.shape[0]
    n_pages = page_table.shape[1]
    n_pool = cache_k.shape[1]
    gp = min(gp, n_pages)
    nj = n_pages // gp
    ck = jnp.transpose(cache_k, (0, 1, 3, 4, 2)).reshape(-1, FOX_W, PAGE)
    cv = jnp.transpose(cache_v, (0, 1, 3, 4, 2)).reshape(-1, FOX_W, PAGE)
    clf = jnp.transpose(cache_lf, (0, 1, 3, 2)).reshape(-1, FOX_H, PAGE)
    base = layer * n_pool
    u_mat = jnp.asarray(np.tril(np.ones((PAGE, PAGE)), -1), BF16)
    eye = jnp.eye(FOX_H, dtype=BF16)
    qbd = (eye[None, :, :, None] * qn_bf.reshape(bd, 1, FOX_H, FOX_D)).reshape(bd, FOX_H, FOX_W)
    row = lambda w: pl.BlockSpec((1, 1, w), lambda b, j, pt: (b, 0, 0))
    full = lambda *s: pl.BlockSpec(s, lambda b, j, pt: (0,) * len(s))

    def page_spec(w, g):
        return pl.BlockSpec((1, w, PAGE),
                            lambda b, j, pt: (base + pt[b, n_pages - 1 - (j * gp + g)], 0, 0))

    col = lambda w: pl.BlockSpec((1, w, 1), lambda b, j, pt: (b, 0, 0))
    in_specs = ([pl.BlockSpec((1, FOX_H, FOX_W), lambda b, j, pt: (b, 0, 0)),
                 col(FOX_H), row(FOX_W), col(FOX_W), col(FOX_W), full(PAGE, PAGE)]
                + [page_spec(FOX_W, g) for g in range(gp)]
                + [page_spec(FOX_W, g) for g in range(gp)]
                + [page_spec(FOX_H, g) for g in range(gp)])
    c3 = lambda a: a.astype(F32).reshape(bd, -1, 1)
    out = pl.pallas_call(
        functools.partial(_fox_dec_kernel, gp=gp),
        grid_spec=pltpu.PrefetchScalarGridSpec(
            num_scalar_prefetch=1,
            grid=(bd, nj),
            in_specs=in_specs,
            out_specs=col(FOX_W),
            scratch_shapes=[pltpu.VMEM((FOX_H, 1), F32), pltpu.VMEM((FOX_H, 1), F32),
                            pltpu.VMEM((FOX_W, PAGE), F32), pltpu.VMEM((FOX_H, 1), F32)]),
        out_shape=jax.ShapeDtypeStruct((bd, FOX_W, 1), F32),
        compiler_params=_cparams(("parallel", "arbitrary")),
        name="fox_decode",
    )(page_table, qbd, c3(logf8), kn.reshape(bd, 1, FOX_W), c3(v), c3(gate), u_mat,
      *([ck] * gp), *([cv] * gp), *([clf] * gp))
    return out.reshape(bd, FOX_W)


def _mla_dec_kernel(pt_ref, qa_ref, qr_ref, cnew_ref, pnew_ref, wkt_ref, *rest, gp):
    c_refs = rest[0:gp]
    p_refs = rest[gp:2 * gp]
    o_ref, m_ref, l_ref, acc_ref = rest[2 * gp:]
    j = pl.program_id(1)
    nj = pl.num_programs(1)

    @pl.when(j == 0)
    def _():
        m_ref[...] = jnp.full_like(m_ref, NEG_INF)
        l_ref[...] = jnp.zeros_like(l_ref)
        acc_ref[...] = jnp.zeros_like(acc_ref)

    qa = qa_ref[0]
    qr = qr_ref[0]

    def scores_t(cb, pt):
        n = cb.shape[0]
        kn = _dot_nt(wkt_ref[...], cb)
        ssq = jnp.sum((kn * kn).reshape(MLA_H, MLA_NOPE, n), axis=1)
        ssq = ssq + jnp.sum(pt * pt, axis=0, keepdims=True)
        raw = _dot_nt(qa, cb) + _dot(qr, pt.astype(BF16))
        return raw * lax.rsqrt(ssq * (1.0 / MLA_QK) + RMS_EPS)

    scores = []
    cbs = []
    for g in range(0, gp, 2):
        cb = jnp.concatenate([c_refs[g][0].astype(BF16), c_refs[g + 1][0].astype(BF16)], axis=0)
        pt = jnp.concatenate([p_refs[g][0], p_refs[g + 1][0]], axis=1)
        scores.append(scores_t(cb, pt))
        cbs.append(cb)
    m_prev = m_ref[...]
    m_new = m_prev
    for s in scores:
        m_new = jnp.maximum(m_new, jnp.max(s, axis=1, keepdims=True))
    alpha = jnp.exp(m_prev - m_new)
    l_new = alpha * l_ref[...]
    acc = alpha * acc_ref[...]
    for s, cb in zip(scores, cbs):
        p = jnp.exp(s - m_new)
        l_new = l_new + jnp.sum(p, axis=1, keepdims=True)
        acc = acc + _dot(p.astype(BF16), cb)
    m_ref[...] = m_new
    l_ref[...] = l_new
    acc_ref[...] = acc

    @pl.when(j == nj - 1)
    def _():
        cnew = cnew_ref[0]
        cblk = jnp.broadcast_to(cnew, (PAGE, MLA_KV_LORA)).astype(BF16)
        pblk = jnp.broadcast_to(pnew_ref[0], (MLA_ROPE, PAGE))
        s_new = scores_t(cblk, pblk)[:, 0:1]
        m_fin = jnp.maximum(m_new, s_new)
        a2 = jnp.exp(m_new - m_fin)
        p_new = jnp.exp(s_new - m_fin)
        l_fin = a2 * l_new + p_new
        num = a2 * acc + p_new * cnew
        o_ref[0] = num / l_fin


def _mla_decode(qa, qr, c_new, kpe_new, wkt_bf16, cache_c, cache_p, layer, page_table, *, gp=32):
    bd = qa.shape[0]
    n_pages = page_table.shape[1]
    n_pool = cache_c.shape[1]
    gp = min(gp, n_pages)
    nj = n_pages // gp
    cc = cache_c.reshape(-1, PAGE, MLA_KV_LORA)
    cp = jnp.transpose(cache_p, (0, 1, 3, 2)).reshape(-1, MLA_ROPE, PAGE)
    base = layer * n_pool
    full = lambda *s: pl.BlockSpec(s, lambda b, j, pt: (0,) * len(s))
    per_b = lambda *s: pl.BlockSpec((1,) + s, lambda b, j, pt: (b, 0, 0))

    def page_spec(r, c, g):
        return pl.BlockSpec((1, r, c), lambda b, j, pt: (base + pt[b, j * gp + g], 0, 0))

    in_specs = ([per_b(MLA_H, MLA_KV_LORA), per_b(MLA_H, MLA_ROPE), per_b(1, MLA_KV_LORA), per_b(MLA_ROPE, 1),
                 full(MLA_H * MLA_NOPE, MLA_KV_LORA)]
                + [page_spec(PAGE, MLA_KV_LORA, g) for g in range(gp)]
                + [page_spec(MLA_ROPE, PAGE, g) for g in range(gp)])
    lat = pl.pallas_call(
        functools.partial(_mla_dec_kernel, gp=gp),
        grid_spec=pltpu.PrefetchScalarGridSpec(
            num_scalar_prefetch=1,
            grid=(bd, nj),
            in_specs=in_specs,
            out_specs=per_b(MLA_H, MLA_KV_LORA),
            scratch_shapes=[pltpu.VMEM((MLA_H, 1), F32), pltpu.VMEM((MLA_H, 1), F32),
                            pltpu.VMEM((MLA_H, MLA_KV_LORA), F32)]),
        out_shape=jax.ShapeDtypeStruct((bd, MLA_H, MLA_KV_LORA), F32),
        compiler_params=_cparams(("parallel", "arbitrary")),
        name="mla_decode",
    )(page_table, qa, qr, c_new.reshape(bd, 1, MLA_KV_LORA), kpe_new.reshape(bd, MLA_ROPE, 1),
      wkt_bf16, *([cc] * gp), *([cp] * gp))
    return lat


def _latent_out_kernel(lat_ref, wv_ref, g_ref, o_ref):
    for h in range(MLA_H):
        o = _dot(lat_ref[:, h * MLA_KV_LORA:(h + 1) * MLA_KV_LORA].astype(BF16), wv_ref[h])
        sl = slice(h * MLA_V, (h + 1) * MLA_V)
        o_ref[:, sl] = o * _silu(g_ref[:, sl])


def _latent_out(lat, wv_bf16, gate):
    bd = lat.shape[0]
    return pl.pallas_call(
        _latent_out_kernel,
        out_shape=jax.ShapeDtypeStruct((bd, MLA_W), F32),
        compiler_params=pltpu.CompilerParams(vmem_limit_bytes=VMEM_LIMIT),
        name="latent_out",
    )(lat.reshape(bd, MLA_H * MLA_KV_LORA), wv_bf16, gate)


def _out_kernel(*refs, widths):
    n = len(widths)
    part_refs = refs[:n]
    w_ref, x_ref, o_ref = refs[n:]
    acc = x_ref[...]
    off = 0
    for p_ref, wd in zip(part_refs, widths):
        acc = acc + _dot(p_ref[...].astype(BF16), w_ref[off:off + wd, :])
        off += wd
    o_ref[...] = acc


def _out_proj(parts, w_bf16, x, *, tm=512):
    m, d = x.shape
    tm = min(tm, m)
    widths = tuple(p.shape[1] for p in parts)
    row = lambda w: pl.BlockSpec((tm, w), lambda i: (i, 0))
    return pl.pallas_call(
        functools.partial(_out_kernel, widths=widths),
        grid=(m // tm,),
        in_specs=[row(w) for w in widths] + [pl.BlockSpec(w_bf16.shape, lambda i: (0, 0)), row(d)],
        out_specs=row(d),
        out_shape=jax.ShapeDtypeStruct((m, d), F32),
        compiler_params=_cparams(("parallel",)),
        name="out_proj",
    )(*parts, w_bf16, x)


def _even_weights(w_in, b_f, q_gain, k_gain, w_g2):
    o = np.cumsum((0, FOX_W, FOX_W, FOX_W, FOX_H, FOX_W, GLA_KW, GLA_KW, GLA_W, GLA_RANK, GLA_W))
    seg = lambda i: w_in[:, o[i]:o[i + 1]]
    pad = jnp.zeros((w_in.shape[0], 128 - FOX_H - GLA_RANK), w_in.dtype)
    w = jnp.concatenate([seg(0), seg(1), seg(2), seg(4), seg(5), seg(6), seg(7), seg(9), seg(3), seg(8), pad],
                        axis=1).astype(BF16)
    bias128 = jnp.zeros((1, 128), F32).at[0, :FOX_H].set(b_f)
    qg = jnp.tile(q_gain, FOX_H).reshape(1, FOX_W)
    kg = jnp.tile(k_gain, FOX_H).reshape(1, FOX_W)
    w2pad = jnp.zeros((128, GLA_KW), F32).at[FOX_H:FOX_H + GLA_RANK].set(w_g2).astype(BF16)
    return w, bias128, qg, kg, w2pad


def _even_layer(xp, xs, e, cache_k, cache_v, cache_lf, state, page_table, norm, w_in, b_f, q_gain, k_gain,
                w_g2, b_g, out_gain, w_out):
    bp, tp, d = xp.shape
    bd = xs.shape[0]
    w, bias128, qg, kg, w2pad = _even_weights(w_in, b_f, q_gain, k_gain, w_g2)
    bg = b_g.reshape(1, GLA_KW)
    gain = out_gain.reshape(1, GLA_DV)
    w_out_bf = w_out.astype(BF16)

    x2 = xp.reshape(bp * tp, d)
    qa, ka, kn, v, vt, rest, misc = _even_in(x2, norm, w, bias128, qg, kg, tp, FOX_SCALE * LOG2E)
    r3 = lambda a: a.reshape(bp, tp, a.shape[-1])
    rest3 = r3(rest)
    fox = _flash(r3(qa), r3(ka), vt, rest3, 0, FOX_H, FOX_D, "flash_fox")
    gla, sfin = _gla_prompt(rest3, r3(misc), w2pad, bg, gain)
    xp_new = _out_proj([fox.reshape(bp * tp, FOX_W), gla.reshape(bp * tp, GLA_W)], w_out_bf, x2).reshape(bp, tp, d)
    tok_major = lambda a: a.reshape(bp, FOX_H, FOX_D, tp).transpose(0, 3, 1, 2)
    outs_p = (tok_major(kn), tok_major(v),
              r3(misc)[:, :, :FOX_H], sfin.transpose(0, 1, 3, 2))

    xs2 = xs.reshape(bd, d)
    qa, ka, kn, v, vt, rest, misc = _even_in(xs2, norm, w, bias128, qg, kg, 1, FOX_SCALE)
    qn = qa.reshape(bd, FOX_H, HP)[:, :, :FOX_D].reshape(bd, FOX_W)
    fox = _fox_decode(qn, kn, v, misc[:, :FOX_H], rest[:, :FOX_W], cache_k, cache_v, cache_lf, e, page_table)
    gla, snew = _gla_step(rest, misc, state, w_g2, b_g, gain)
    xs_new = _out_proj([fox, gla], w_out_bf, xs2).reshape(bd, 1, d)
    outs_s = (kn.reshape(bd, 1, FOX_H, FOX_D), v.reshape(bd, 1, FOX_H, FOX_D),
              misc[:, :FOX_H].reshape(bd, 1, FOX_H), snew)
    return xp_new, xs_new, outs_p, outs_s


def _odd_weights(w_in, w_q_b, w_kv_b, q_gain, k_gain):
    rows = w_in.shape[0]
    o = np.cumsum((0, MLA_Q_LORA, MLA_KV_LORA, MLA_ROPE, MLA_W))
    z = lambda n: jnp.zeros((rows, n), w_in.dtype)
    w = jnp.concatenate([w_in[:, o[0]:o[1]], w_in[:, o[1]:o[2]], z(ROPE_LO), w_in[:, o[2]:o[3]],
                         z(HP - ROPE_LO - MLA_ROPE), w_in[:, o[3]:o[4]]], axis=1).astype(BF16)
    wq = jnp.pad(w_q_b.reshape(MLA_Q_LORA, MLA_H, MLA_QK), ((0, 0), (0, 0), (0, HP - MLA_QK)))
    wq = wq.reshape(MLA_Q_LORA, MLA_H * HP).astype(BF16)
    wkv = w_kv_b.reshape(MLA_KV_LORA, MLA_H, MLA_NOPE + MLA_V)
    wk = wkv[:, :, :MLA_NOPE]
    wk_pad = jnp.pad(wk, ((0, 0), (0, 0), (0, HP - MLA_NOPE))).reshape(MLA_KV_LORA, MLA_H * HP).astype(BF16)
    wv = wkv[:, :, MLA_NOPE:]
    padg = lambda g: jnp.pad(g, (0, HP - MLA_QK)).reshape(1, HP)
    return w, wq, wk, wk_pad, wv, padg(q_gain), padg(k_gain)


def _rope_tiles(pos):
    inv = ROPE_THETA ** (-jnp.arange(ROPE_HALF, dtype=F32) / ROPE_HALF)
    ang = pos.astype(F32)[:, None] * inv[None, :]
    cos, sin = jnp.cos(ang), jnp.sin(ang)
    n = pos.shape[0]
    one = jnp.ones((n, ROPE_LO), F32)
    z = lambda w: jnp.zeros((n, w), F32)
    tail = HP - ROPE_LO - MLA_ROPE
    cos_t = jnp.concatenate([one, cos, cos, z(tail)], axis=1)
    sa_t = jnp.concatenate([z(ROPE_LO + ROPE_HALF), sin, z(tail)], axis=1)
    sb_t = jnp.concatenate([z(ROPE_LO), -sin, z(ROPE_HALF + tail)], axis=1)
    return cos_t, sa_t, sb_t


def _odd_layer(xp, xs, o, cache_c, cache_p, page_table, norm, w_in, q_a_gain, w_q_b, kv_a_gain, w_kv_b,
               q_gain, k_gain, w_out):
    bp, tp, d = xp.shape
    bd = xs.shape[0]
    past = page_table.shape[1] * PAGE
    w, wq, wk, wk_pad, wv, qg, kg = _odd_weights(w_in, w_q_b, w_kv_b, q_gain, k_gain)
    wv_flat = wv.reshape(MLA_KV_LORA, MLA_W).astype(BF16)
    w_out_bf = w_out.astype(BF16)

    x2 = xp.reshape(bp * tp, d)
    q, k, vt, c, kpe, gate = _odd_in(x2, norm, w, q_a_gain, wq, kv_a_gain, wk_pad, wv_flat, qg, kg,
                                     _rope_tiles(jnp.arange(tp)), MLA_SCALE * LOG2E)
    r3 = lambda a: a.reshape(bp, tp, a.shape[-1])
    att = _flash(r3(q), r3(k), vt, r3(gate), 0, MLA_H, MLA_V, "flash_mla")
    xp_new = _out_proj([att.reshape(bp * tp, MLA_W)], w_out_bf, x2).reshape(bp, tp, d)
    outs_p = (c.reshape(bp, tp, MLA_KV_LORA), kpe.reshape(bp, tp, MLA_ROPE))

    xs2 = xs.reshape(bd, d)
    q, _, _, c, kpe, gate = _odd_in(xs2, norm, w, q_a_gain, wq, kv_a_gain, wk_pad, wv_flat, qg, kg,
                                    _rope_tiles(jnp.full((bd,), past, jnp.int32)), MLA_SCALE)
    qk = q.astype(F32).reshape(bd, MLA_H, HP)[:, :, :MLA_QK] * k_gain
    qa = jnp.einsum('bhd,rhd->bhr', qk[:, :, :MLA_NOPE], wk).astype(BF16)
    qr = qk[:, :, MLA_NOPE:].astype(BF16)
    wkt = wk.reshape(MLA_KV_LORA, MLA_H * MLA_NOPE).T.astype(BF16)
    lat = _mla_decode(qa, qr, c, kpe, wkt, cache_c, cache_p, o, page_table)
    att = _latent_out(lat, wv.transpose(1, 0, 2).astype(BF16), gate)
    xs_new = _out_proj([att], w_out_bf, xs2).reshape(bd, 1, d)
    outs_s = (c.reshape(bd, 1, MLA_KV_LORA), kpe.reshape(bd, 1, MLA_ROPE))
    return xp_new, xs_new, outs_p, outs_s


def kernel(x_prompt, x_sample, cache_fox_k, cache_fox_v, cache_fox_logf, cache_mla_ckv, cache_mla_kpe,
           state_gla, page_table, norm_even, w_in_even, b_fox_f, fox_q_gain, fox_k_gain, gla_w_gate2,
           gla_b_gate, gla_out_gain, w_out_even, norm_odd, w_in_odd, mla_q_a_gain, w_q_b, mla_kv_a_gain,
           w_kv_b, mla_q_gain, mla_k_gain, w_out_odd):
    depth = norm_even.shape[0] + norm_odd.shape[0]
    xp, xs = x_prompt, x_sample
    ev_p, ev_s, od_p, od_s = [], [], [], []
    for layer in range(depth):
        i = layer // 2
        if layer % 2 == 0:
            xp, xs, op, os_ = _even_layer(
                xp, xs, i, cache_fox_k, cache_fox_v, cache_fox_logf, state_gla[i], page_table,
                norm_even[i], w_in_even[i], b_fox_f[i], fox_q_gain[i], fox_k_gain[i], gla_w_gate2[i],
                gla_b_gate[i], gla_out_gain[i], w_out_even[i])
            ev_p.append(op)
            ev_s.append(os_)
        else:
            xp, xs, op, os_ = _odd_layer(
                xp, xs, i, cache_mla_ckv, cache_mla_kpe, page_table, norm_odd[i], w_in_odd[i],
                mla_q_a_gain[i], w_q_b[i], mla_kv_a_gain[i], w_kv_b[i], mla_q_gain[i], mla_k_gain[i],
                w_out_odd[i])
            od_p.append(op)
            od_s.append(os_)
    st = lambda lst, k: jnp.stack([t[k] for t in lst])
    return (xp, xs,
            st(ev_p, 0), st(ev_p, 1), st(ev_p, 2), st(od_p, 0), st(od_p, 1), st(ev_p, 3),
            st(ev_s, 0), st(ev_s, 1), st(ev_s, 2), st(od_s, 0), st(od_s, 1), st(ev_s, 3))
```

```python
import functools

import numpy as np
import jax
import jax.numpy as jnp
from jax import lax
from jax.experimental import pallas as pl
from jax.experimental.pallas import tpu as pltpu

F32 = jnp.float32
BF16 = jnp.bfloat16

D_MODEL = 1024
PAGE = 128
RMS_EPS = 1e-6
NEG_INF = -1e30
FOX_H = 8
FOX_D = 64
FOX_W = FOX_H * FOX_D
FOX_SCALE = FOX_D ** -0.5
GLA_H = 4
GLA_DK = 64
GLA_DV = 128
GLA_KW = GLA_H * GLA_DK
GLA_W = GLA_H * GLA_DV
GLA_RANK = 16
GLA_GATE_NORM = 16.0
GLA_CHUNK = 64
MLA_H = 16
MLA_Q_LORA = 256
MLA_KV_LORA = 128
MLA_NOPE = 64
MLA_ROPE = 32
MLA_QK = MLA_NOPE + MLA_ROPE
MLA_V = 64
MLA_W = MLA_H * MLA_V
MLA_SCALE = MLA_QK ** -0.5
HP = 128
ROPE_THETA = 10000.0
LOG2E = 1.4426950408889634

VMEM_LIMIT = 56 * 1024 * 1024

E_Q, E_K, E_V, E_REST, E_MISC, E_END = 0, 512, 1024, 1536, 3584, 3712


def _cparams(sem):
    return pltpu.CompilerParams(dimension_semantics=sem, vmem_limit_bytes=VMEM_LIMIT)


def _log_sigmoid(x):
    return jnp.minimum(x, 0.0) - jnp.log1p(jnp.exp(-jnp.abs(x)))


def _silu(x):
    return x / (1.0 + jnp.exp(-x))


def _split_bf16(x):
    hi = x.astype(BF16)
    lo = (x - hi.astype(F32)).astype(BF16)
    return hi, lo


def _dot(a, b):
    return jnp.dot(a, b, preferred_element_type=F32)


def _dot_nt(a, b):
    return lax.dot_general(a, b, (((1,), (1,)), ((), ())), preferred_element_type=F32)


def _split3_bf16(x):
    hi = x.astype(BF16)
    r = x - hi.astype(F32)
    mid = r.astype(BF16)
    lo = (r - mid.astype(F32)).astype(BF16)
    return hi, mid, lo


def _rmsnorm(x, gain):
    return x * lax.rsqrt(jnp.mean(x * x, axis=-1, keepdims=True) + RMS_EPS) * gain


O_QA, O_KVA, O_KPE, O_G, O_END = 0, 256, 384, 512, 1536
ROPE_LO = MLA_NOPE
ROPE_HALF = MLA_ROPE // 2


def _odd_in_kernel(x_ref, g_ref, w_ref, qag_ref, wq_ref, kvg_ref, wk_ref, wv_ref, qg_ref, kg_ref, ones_ref,
                   cos_ref, sa_ref, sb_ref, q_ref, k_ref, vt_ref, c_ref, kpe_ref, gate_ref, *, q_scale):
    xn = _rmsnorm(x_ref[...], g_ref[...]).astype(BF16)

    def mm(lo, hi):
        return _dot(xn, w_ref[:, lo:hi])

    cos, sa, sb = cos_ref[...], sa_ref[...], sb_ref[...]

    def rope(t):
        return (t * cos + pltpu.roll(t, ROPE_HALF, 1) * sa + pltpu.roll(t, HP - ROPE_HALF, 1) * sb)

    def headnorm(t, gain):
        ss = _dot((t * t).astype(BF16), ones_ref[...])
        return t * lax.rsqrt(ss * (1.0 / MLA_QK) + RMS_EPS) * gain

    gate_ref[...] = mm(O_G, O_END)
    c = _rmsnorm(mm(O_KVA, O_KPE), kvg_ref[...])
    c_ref[...] = c
    kpe_t = rope(mm(O_KPE, O_G))
    kpe_ref[...] = kpe_t[:, ROPE_LO:ROPE_LO + MLA_ROPE]
    qa = _rmsnorm(mm(O_QA, O_KVA), qag_ref[...]).astype(BF16)
    cb = c.astype(BF16)
    for h in range(MLA_H):
        sl = slice(h * HP, (h + 1) * HP)
        qt = rope(_dot(qa, wq_ref[:, sl]))
        q_ref[:, sl] = (headnorm(qt, qg_ref[...]) * q_scale).astype(BF16)
        kt = _dot(cb, wk_ref[:, sl]) + kpe_t
        k_ref[:, sl] = headnorm(kt, kg_ref[...]).astype(BF16)
    vt_ref[...] = _dot(cb, wv_ref[...]).T.astype(BF16)


def _odd_in(x, gain, w, qag, wq, kvg, wk, wv, qg, kg, tables, q_scale, *, tm=256):
    m, d = x.shape
    tm = min(tm, m)
    cos_t, sa_t, sb_t = tables
    n_tab = cos_t.shape[0] // tm
    row = lambda wd: pl.BlockSpec((tm, wd), lambda i: (i, 0))
    full = lambda a, b: pl.BlockSpec((a, b), lambda i: (0, 0))
    tab = pl.BlockSpec((tm, HP), lambda i: (i % n_tab, 0))
    ones = jnp.ones((HP, HP), BF16)
    return pl.pallas_call(
        functools.partial(_odd_in_kernel, q_scale=q_scale),
        grid=(m // tm,),
        in_specs=[row(d), full(1, d), full(d, O_END), full(1, MLA_Q_LORA), full(MLA_Q_LORA, MLA_H * HP),
                  full(1, MLA_KV_LORA), full(MLA_KV_LORA, MLA_H * HP), full(MLA_KV_LORA, MLA_W),
                  full(1, HP), full(1, HP), full(HP, HP), tab, tab, tab],
        out_specs=[row(MLA_H * HP), row(MLA_H * HP), pl.BlockSpec((MLA_W, tm), lambda i: (0, i)),
                   row(MLA_KV_LORA), row(MLA_ROPE), row(MLA_W)],
        out_shape=[jax.ShapeDtypeStruct((m, MLA_H * HP), BF16),
                   jax.ShapeDtypeStruct((m, MLA_H * HP), BF16),
                   jax.ShapeDtypeStruct((MLA_W, m), BF16),
                   jax.ShapeDtypeStruct((m, MLA_KV_LORA), F32),
                   jax.ShapeDtypeStruct((m, MLA_ROPE), F32),
                   jax.ShapeDtypeStruct((m, MLA_W), F32)],
        compiler_params=_cparams(("parallel",)),
        name="odd_in_proj",
    )(x, gain.reshape(1, d), w, qag.reshape(1, -1), wq, kvg.reshape(1, -1), wk, wv, qg, kg, ones,
      cos_t, sa_t, sb_t)


AUG_Q = FOX_D
AUG_K = FOX_D + 3


def _even_in_kernel(x_ref, g_ref, w_ref, bias_ref, qg_ref, kg_ref, ones_ref, tri_ref, place_ref, pc_ref, aug_ref,
                    qa_ref, ka_ref, kn_ref, v_ref, vt_ref, rest_ref, misc_ref,
                    carry_ref, *, tiles_per_seq, do_cum, q_scale):
    xn = _rmsnorm(x_ref[...], g_ref[...]).astype(BF16)

    def mm(lo, hi):
        return _dot(xn, w_ref[:, lo:hi])

    def headnorm(z, gain):
        ss = _dot((z * z).astype(BF16), ones_ref[...])
        return z * lax.rsqrt(ss * (1.0 / FOX_D) + RMS_EPS) * gain

    q = headnorm(mm(E_Q, E_K), qg_ref[...]) * q_scale
    k = headnorm(mm(E_K, E_V), kg_ref[...])
    v = mm(E_V, E_REST)
    vt = v.T
    vt_ref[...] = vt.astype(BF16)
    if do_cum:
        kn_ref[0] = k.T
        v_ref[0] = vt
    else:
        kn_ref[...] = k
        v_ref[...] = v
    rest_ref[...] = mm(E_REST, E_MISC)
    mz = mm(E_MISC, E_END) + bias_ref[...]
    lane = lax.broadcasted_iota(jnp.int32, mz.shape, 1)
    logf = jnp.where(lane < FOX_H, _log_sigmoid(mz), mz)
    misc_ref[...] = logf
    qa = _dot(q.astype(BF16), place_ref[...])
    ka = _dot(k.astype(BF16), place_ref[...])
    if do_cum:
        @pl.when(pl.program_id(0) % tiles_per_seq == 0)
        def _():
            carry_ref[...] = jnp.zeros_like(carry_ref)

        lf = jnp.where(lane < FOX_H, logf, 0.0)
        hi, lo = _split_bf16(lf)
        c = _dot(tri_ref[...], hi) + _dot(tri_ref[...], lo) + carry_ref[...]
        carry_ref[...] = c[c.shape[0] - 1:, :]
        pieces = _split3_bf16(c * LOG2E)
        for p in range(3):
            qa = qa + _dot(pieces[p], pc_ref[p])
            ka = ka + _dot(pieces[p], pc_ref[3 + p])
        qa = qa + aug_ref[0:1, :]
        ka = ka + aug_ref[1:2, :]
    qa_ref[...] = qa.astype(BF16)
    ka_ref[...] = ka.astype(BF16)


def _even_in(x, gain, w_bf16, bias128, qg, kg, seq_len, q_scale, *, tm=256):
    m, k = x.shape
    tm = min(tm, m, seq_len) if seq_len > 1 else min(tm, m)
    do_cum = seq_len > 1
    tiles_per_seq = max(seq_len // tm, 1)
    ones_bd = jnp.asarray(np.kron(np.eye(FOX_H), np.ones((FOX_D, FOX_D))), BF16)
    tri = jnp.asarray(np.tril(np.ones((tm, tm))), BF16)
    place = np.zeros((FOX_H, FOX_D, FOX_H, HP), np.float32)
    pc = np.zeros((6, 128, FOX_H, HP), np.float32)
    aug = np.zeros((2, FOX_H, HP), np.float32)
    for h in range(FOX_H):
        place[h, np.arange(FOX_D), h, np.arange(FOX_D)] = 1.0
        for p in range(3):
            pc[p, h, h, AUG_Q + p] = 1.0
            pc[3 + p, h, h, AUG_K + p] = -1.0
        aug[0, h, AUG_K:AUG_K + 3] = 1.0
        aug[1, h, AUG_Q:AUG_Q + 3] = 1.0
    place = jnp.asarray(place.reshape(FOX_W, FOX_H * HP), BF16)
    pc = jnp.asarray(pc.reshape(6, 128, FOX_H * HP), BF16)
    aug = jnp.asarray(aug.reshape(2, FOX_H * HP), F32)
    row = lambda w: pl.BlockSpec((tm, w), lambda i: (i, 0))
    full = lambda *s: pl.BlockSpec(s, lambda i: (0,) * len(s))
    if do_cum:
        kv_spec = pl.BlockSpec((1, FOX_W, tm), lambda i: (i // tiles_per_seq, 0, i % tiles_per_seq))
        kv_shape = jax.ShapeDtypeStruct((m // seq_len, FOX_W, seq_len), F32)
    else:
        kv_spec = row(FOX_W)
        kv_shape = jax.ShapeDtypeStruct((m, FOX_W), F32)
    outs = pl.pallas_call(
        functools.partial(_even_in_kernel, tiles_per_seq=tiles_per_seq, do_cum=do_cum, q_scale=q_scale),
        grid=(m // tm,),
        in_specs=[row(k), full(1, k), full(k, E_END), full(1, 128), full(1, FOX_W), full(1, FOX_W),
                  full(FOX_W, FOX_W), full(tm, tm), full(FOX_W, FOX_H * HP), full(6, 128, FOX_H * HP),
                  full(2, FOX_H * HP)],
        out_specs=[row(FOX_H * HP), row(FOX_H * HP), kv_spec, kv_spec,
                   pl.BlockSpec((FOX_W, tm), lambda i: (0, i)),
                   row(E_MISC - E_REST), row(128)],
        out_shape=[jax.ShapeDtypeStruct((m, FOX_H * HP), BF16),
                   jax.ShapeDtypeStruct((m, FOX_H * HP), BF16),
                   kv_shape, kv_shape,
                   jax.ShapeDtypeStruct((FOX_W, m), BF16),
                   jax.ShapeDtypeStruct((m, E_MISC - E_REST), F32),
                   jax.ShapeDtypeStruct((m, 128), F32)],
        scratch_shapes=[pltpu.VMEM((1, 128), F32)],
        compiler_params=_cparams(("arbitrary",)),
        name="even_in_proj",
    )(x, gain.reshape(1, k), w_bf16, bias128, qg, kg, ones_bd, tri, place, pc, aug)
    return outs


ONES_ROWS = 16


def _flash_kernel(qi_ref, kj_ref, q_ref, k_ref, vt_ref, g_ref, o_ref, m_ref, acc_ref, *, n_heads, dv):
    i = qi_ref[pl.program_id(1)]
    j = kj_ref[pl.program_id(1)]

    @pl.when(j == 0)
    def _():
        m_ref[...] = jnp.full_like(m_ref, NEG_INF)
        acc_ref[...] = jnp.zeros_like(acc_ref)

    def step(masked):
        tq = q_ref.shape[1]
        tk = k_ref.shape[1]
        ones = jnp.ones((ONES_ROWS, tk), BF16)
        if masked:
            r = lax.broadcasted_iota(jnp.int32, (tk, tq), 0)
            c = lax.broadcasted_iota(jnp.int32, (tk, tq), 1)
            keep = c >= r
        for h in range(n_heads):
            q = q_ref[0, :, h * HP:(h + 1) * HP]
            k = k_ref[0, :, h * HP:(h + 1) * HP]
            s = _dot_nt(k, q)
            if masked:
                s = jnp.where(keep, s, NEG_INF)
            m_prev = m_ref[h]
            m_new = jnp.maximum(m_prev, jnp.max(s, axis=0, keepdims=True))
            alpha = jnp.exp2(m_prev - m_new)
            p = jnp.exp2(s - m_new)
            va = jnp.concatenate([vt_ref[h * dv:(h + 1) * dv, :], ones], axis=0)
            acc_ref[h] = alpha * acc_ref[h] + _dot(va, p.astype(BF16))
            m_ref[h] = m_new

    @pl.when(j < i)
    def _():
        step(False)

    @pl.when(j == i)
    def _():
        step(True)
        for h in range(0, n_heads, 2):
            sl = slice(h * dv, (h + 2) * dv)
            a0, a1 = acc_ref[h], acc_ref[h + 1]
            o = jnp.concatenate([a0[:dv] / a0[dv:dv + 1], a1[:dv] / a1[dv:dv + 1]], axis=0)
            o_ref[0, :, sl] = (o.T * _silu(g_ref[0, :, sl])).astype(o_ref.dtype)


def _flash(q, k, vt, gate, gate_block, n_heads, dv, name, *, t_blk=512):
    b, t, _ = q.shape
    tb = min(t_blk, t)
    n = t // tb
    pairs = [(i, j) for i in range(n) for j in range(i + 1)]
    qi = jnp.asarray([p[0] for p in pairs], jnp.int32)
    kj = jnp.asarray([p[1] for p in pairs], jnp.int32)
    qmap = lambda bi, s, qi, kj: (bi, qi[s], 0)
    kmap = lambda bi, s, qi, kj: (bi, kj[s], 0)
    in_specs = [pl.BlockSpec((1, tb, n_heads * HP), qmap),
                pl.BlockSpec((1, tb, n_heads * HP), kmap),
                pl.BlockSpec((n_heads * dv, tb), lambda bi, s, qi, kj: (0, bi * n + kj[s])),
                pl.BlockSpec((1, tb, n_heads * dv), lambda bi, s, qi, kj: (bi, qi[s], gate_block))]
    return pl.pallas_call(
        functools.partial(_flash_kernel, n_heads=n_heads, dv=dv),
        grid_spec=pltpu.PrefetchScalarGridSpec(
            num_scalar_prefetch=2,
            grid=(b, len(pairs)),
            in_specs=in_specs,
            out_specs=pl.BlockSpec((1, tb, n_heads * dv), qmap),
            scratch_shapes=[pltpu.VMEM((n_heads, 1, tb), F32),
                            pltpu.VMEM((n_heads, dv + ONES_ROWS, tb), F32)]),
        out_shape=jax.ShapeDtypeStruct((b, t, n_heads * dv), BF16),
        compiler_params=_cparams(("parallel", "arbitrary")),
        name=name,
    )(qi, kj, q, k, vt, gate)


PAIR_BLK = 16


def _gla_gate(misc, w2_ref, bg_ref):
    pre = _dot(misc.astype(BF16), w2_ref[...]) + bg_ref[...]
    return _log_sigmoid(pre) * (1.0 / GLA_GATE_NORM)


def _gla_kernel(q_ref, k_ref, v_ref, gg_ref, misc_ref, w2_ref, bg_ref, gain_ref, hsel_ref, tri_ref,
                o_ref, sfin_ref, st_ref, w_s, *, cps):
    c = pl.program_id(1)
    n_c = pl.num_programs(1)

    @pl.when(c == 0)
    def _():
        st_ref[...] = jnp.zeros_like(st_ref)

    for ci in range(cps):
        rs = pl.ds(ci * GLA_CHUNK, GLA_CHUNK)
        _gla_chunk(q_ref.at[0, rs], k_ref.at[0, rs], v_ref.at[0, rs], gg_ref.at[0, rs], misc_ref.at[0, rs],
                   w2_ref, bg_ref, gain_ref, hsel_ref, tri_ref, o_ref.at[0, rs], st_ref, w_s.at[ci])

    @pl.when(c == n_c - 1)
    def _():
        sfin_ref[0] = st_ref[...]


def _gla_chunk(q_ref, k_ref, v_ref, gg_ref, misc_ref, w2_ref, bg_ref, gain_ref, hsel_ref, tri_ref,
               o_ref, st_ref, w_s):
    ch = GLA_CHUNK
    q = q_ref[...] * (GLA_DK ** -0.5)
    k = k_ref[...]
    v = v_ref[...]
    la = _gla_gate(misc_ref[...], w2_ref, bg_ref)
    hi, lo = _split_bf16(la)
    cum = _dot(tri_ref[...], hi) + _dot(tri_ref[...], lo)
    nb = ch // PAIR_BLK
    blk = lambda a, i: a[i * PAIR_BLK:(i + 1) * PAIR_BLK]

    for j in range(ch):
        ib = j // PAIR_BLK
        w = (jnp.exp(jnp.minimum(blk(cum, ib) - cum[j:j + 1, :], 0.0)) * (blk(q, ib) * k[j:j + 1, :])).astype(BF16)
        off = (j - ib * PAIR_BLK) * 128
        w_s[0, ib * PAIR_BLK:(ib + 1) * PAIR_BLK, off:off + 128] = w[:, 0:128]
        w_s[1, ib * PAIR_BLK:(ib + 1) * PAIR_BLK, off:off + 128] = w[:, 128:256]
    r = lax.broadcasted_iota(jnp.int32, (PAIR_BLK, PAIR_BLK), 0)
    cc = lax.broadcasted_iota(jnp.int32, (PAIR_BLK, PAIR_BLK), 1)
    causal = r >= cc
    diag = [[None] * nb for _ in range(GLA_H)]
    for lt in range(2):
        for ib in range(nb):
            a2 = _dot_nt(w_s[lt, ib * PAIR_BLK:(ib + 1) * PAIR_BLK, :], hsel_ref[...])
            for hh in range(2):
                diag[2 * lt + hh][ib] = jnp.where(causal, a2[:, hh * PAIR_BLK:(hh + 1) * PAIR_BLK], 0.0)
    qes, kes = [None], [None]
    for ib in range(1, nb):
        ref = cum[ib * PAIR_BLK - 1:ib * PAIR_BLK, :]
        qes.append((blk(q, ib) * jnp.exp(blk(cum, ib) - ref)).astype(BF16))
        kes.append((k[:ib * PAIR_BLK] * jnp.exp(ref - cum[:ib * PAIR_BLK])).astype(BF16))

    for lt in range(2):
        for hh in range(2):
            h = 2 * lt + hh
            sk = slice(h * GLA_DK, (h + 1) * GLA_DK)
            sv = slice(h * GLA_DV, (h + 1) * GLA_DV)
            vh = v[:, sv]
            vb = vh.astype(BF16)
            rows = []
            for ib in range(nb):
                o_b = _dot(diag[h][ib].astype(BF16), blk(vb, ib))
                if ib > 0:
                    a_off = _dot_nt(qes[ib][:, sk], kes[ib][:, sk])
                    o_b = o_b + _dot(a_off.astype(BF16), vb[:ib * PAIR_BLK])
                rows.append(o_b)
            o_intra = jnp.concatenate(rows, axis=0)
            cum_h = cum[:, sk]
            last = cum_h[ch - 1:, :]
            st = st_ref[h]
            qe = (q[:, sk] * jnp.exp(cum_h)).astype(BF16)
            o = _dot_nt(qe, st.astype(BF16)) + o_intra
            kd = (k[:, sk] * jnp.exp(last - cum_h)).astype(BF16)
            st_ref[h] = st * jnp.exp(last) + _dot(vh.T.astype(BF16), kd)
            ms = jnp.mean(o * o, axis=-1, keepdims=True)
            y = o * lax.rsqrt(ms + RMS_EPS) * gain_ref[...] * _silu(gg_ref[:, sv])
            o_ref[:, sv] = y.astype(o_ref.dtype)


def _gla_prompt(rest, misc, w2pad, bg, gain):
    b, t, _ = rest.shape
    ch = GLA_CHUNK
    n = t // ch
    hsel = np.zeros((2, PAIR_BLK, PAIR_BLK, 2, GLA_DK), np.float32)
    for hh in range(2):
        for j in range(PAIR_BLK):
            hsel[hh, j, j, hh, :] = 1.0
    hsel = jnp.asarray(hsel.reshape(2 * PAIR_BLK, PAIR_BLK * 128), BF16)
    tri = jnp.asarray(np.tril(np.ones((ch, ch))), BF16)
    full = lambda *s: pl.BlockSpec(s, lambda bi, ci: (0,) * len(s))
    cps = 2 if n % 2 == 0 else 1
    rows = cps * ch
    o, sfin = pl.pallas_call(
        functools.partial(_gla_kernel, cps=cps),
        grid=(b, n // cps),
        in_specs=[pl.BlockSpec((1, rows, GLA_KW), lambda bi, ci: (bi, ci, 2)),
                  pl.BlockSpec((1, rows, GLA_KW), lambda bi, ci: (bi, ci, 3)),
                  pl.BlockSpec((1, rows, GLA_W), lambda bi, ci: (bi, ci, 2)),
                  pl.BlockSpec((1, rows, GLA_W), lambda bi, ci: (bi, ci, 3)),
                  pl.BlockSpec((1, rows, 128), lambda bi, ci: (bi, ci, 0)),
                  full(128, GLA_KW), full(1, GLA_KW), full(1, GLA_DV), full(2 * PAIR_BLK, PAIR_BLK * 128),
                  full(ch, ch)],
        out_specs=[pl.BlockSpec((1, rows, GLA_W), lambda bi, ci: (bi, ci, 0)),
                   pl.BlockSpec((1, GLA_H, GLA_DV, GLA_DK), lambda bi, ci: (bi, 0, 0, 0))],
        out_shape=[jax.ShapeDtypeStruct((b, t, GLA_W), BF16),
                   jax.ShapeDtypeStruct((b, GLA_H, GLA_DV, GLA_DK), F32)],
        scratch_shapes=[pltpu.VMEM((GLA_H, GLA_DV, GLA_DK), F32),
                        pltpu.VMEM((cps, 2, ch, PAIR_BLK * 128), BF16)],
        compiler_params=_cparams(("parallel", "arbitrary")),
        name="gla_prompt",
    )(rest, rest, rest, rest, misc, w2pad, bg, gain, hsel, tri)
    return o, sfin


def _gla_step_kernel(qT_ref, kT_ref, gaT_ref, v_ref, gg_ref, s_ref, w2T_ref, bgT_ref, gain_ref,
                     o_ref, snew_ref, *, tb):
    laT = _log_sigmoid(_dot(w2T_ref[...], gaT_ref[0].astype(BF16)) + bgT_ref[...]) * (1.0 / GLA_GATE_NORM)
    eT = jnp.exp(laT)
    for bb in range(tb):
        for h in range(GLA_H):
            sk = slice(h * GLA_DK, (h + 1) * GLA_DK)
            sv = slice(h * GLA_DV, (h + 1) * GLA_DV)
            s_new = s_ref[bb, h] * eT[sk, bb:bb + 1] + kT_ref[0, sk, bb:bb + 1] * v_ref[bb, :, sv]
            snew_ref[bb, h] = s_new
            qcol = qT_ref[0, sk, bb:bb + 1] * (GLA_DK ** -0.5)
            o = jnp.sum(qcol * s_new, axis=0, keepdims=True)
            ms = jnp.mean(o * o, axis=-1, keepdims=True)
            y = o * lax.rsqrt(ms + RMS_EPS) * gain_ref[...] * _silu(gg_ref[bb, :, sv])
            o_ref[bb, :, sv] = y


def _gla_step(rest, misc, state, w2, bg, gain, *, tb=8):
    bd = rest.shape[0]
    tb = min(tb, bd)
    g = bd // tb
    tr = lambda a: a.reshape(g, tb, a.shape[-1]).transpose(0, 2, 1)
    qT = tr(rest[:, 512:768])
    kT = tr(rest[:, 768:1024])
    gaT = tr(misc[:, 8:8 + GLA_RANK])
    v3 = rest[:, 1024:1536].reshape(bd, 1, GLA_W)
    gg3 = rest[:, 1536:2048].reshape(bd, 1, GLA_W)
    blk = lambda *s: pl.BlockSpec(s, lambda i: (i,) + (0,) * (len(s) - 1))
    full = lambda *s: pl.BlockSpec(s, lambda i: (0,) * len(s))
    o, snew = pl.pallas_call(
        functools.partial(_gla_step_kernel, tb=tb),
        grid=(g,),
        in_specs=[blk(1, GLA_KW, tb), blk(1, GLA_KW, tb), blk(1, GLA_RANK, tb),
                  blk(tb, 1, GLA_W), blk(tb, 1, GLA_W), blk(tb, GLA_H, GLA_DK, GLA_DV),
                  full(GLA_KW, GLA_RANK), full(GLA_KW, 1), full(1, GLA_DV)],
        out_specs=[blk(tb, 1, GLA_W), blk(tb, GLA_H, GLA_DK, GLA_DV)],
        out_shape=[jax.ShapeDtypeStruct((bd, 1, GLA_W), F32),
                   jax.ShapeDtypeStruct((bd, GLA_H, GLA_DK, GLA_DV), F32)],
        compiler_params=_cparams(("parallel",)),
        name="gla_step",
    )(qT, kT, gaT, v3, gg3, state, w2.T.astype(BF16), bg.reshape(GLA_KW, 1), gain)
    return o.reshape(bd, GLA_W), snew


def _head_rows(a, n_heads, width):
    return jnp.concatenate([jnp.broadcast_to(a[h:h + 1, :], (width, 1)) for h in range(n_heads)], axis=0)


def _fox_dec_kernel(pt_ref, qbd_ref, cnew_ref, knew_ref, vnew_ref, g_ref, u_ref, *rest, gp):
    k_refs = rest[0:gp]
    v_refs = rest[gp:2 * gp]
    lf_refs = rest[2 * gp:3 * gp]
    o_ref, m_ref, l_ref, acc_ref, sfx_ref = rest[3 * gp:]
    j = pl.program_id(1)
    nj = pl.num_programs(1)

    @pl.when(j == 0)
    def _():
        m_ref[...] = jnp.full_like(m_ref, NEG_INF)
        l_ref[...] = jnp.zeros_like(l_ref)
        acc_ref[...] = jnp.zeros_like(acc_ref)
        sfx_ref[...] = jnp.zeros_like(sfx_ref)

    qbd = qbd_ref[0]
    cnew = cnew_ref[0]
    carry = sfx_ref[...]
    scores = []
    for g in range(gp):
        lf = lf_refs[g][0]
        hi, lo = _split_bf16(lf)
        sfx = _dot(hi, u_ref[...]) + _dot(lo, u_ref[...]) + carry
        carry = carry + jnp.sum(lf, axis=1, keepdims=True)
        s = _dot(qbd, k_refs[g][0].astype(BF16)) + sfx + cnew
        scores.append(s)
    sfx_ref[...] = carry
    m_prev = m_ref[...]
    m_new = m_prev
    for s in scores:
        m_new = jnp.maximum(m_new, jnp.max(s, axis=1, keepdims=True))
    alpha = jnp.exp(m_prev - m_new)
    ps = [jnp.exp(s - m_new) for s in scores]
    l_new = alpha * l_ref[...]
    for p in ps:
        l_new = l_new + jnp.sum(p, axis=1, keepdims=True)
    for h in range(FOX_H):
        rows = slice(h * FOX_D, (h + 1) * FOX_D)
        a = acc_ref[rows, :] * alpha[h:h + 1, :]
        for g in range(gp):
            a = a + ps[g][h:h + 1, :] * v_refs[g][0, rows, :]
        acc_ref[rows, :] = a
    m_ref[...] = m_new
    l_ref[...] = l_new

    @pl.when(j == nj - 1)
    def _():
        s_new = jnp.sum(qbd.astype(F32) * knew_ref[0], axis=1, keepdims=True)
        m_fin = jnp.maximum(m_new, s_new)
        a2 = jnp.exp(m_new - m_fin)
        p_new = jnp.exp(s_new - m_fin)
        l_fin = a2 * l_new + p_new
        tot = jnp.sum(acc_ref[...], axis=1, keepdims=True)
        num = _head_rows(a2, FOX_H, FOX_D) * tot + _head_rows(p_new, FOX_H, FOX_D) * vnew_ref[0]
        o = num / _head_rows(l_fin, FOX_H, FOX_D)
        o_ref[0] = o * _silu(g_ref[0])


def _fox_decode(qn_bf, kn, v, logf8, gate, cache_k, cache_v, cache_lf, layer, page_table, *, gp=16):
    bd = qn_bf.shape[0]
    n_pages = page_table.shape[1]
    n_pool = cache_k.shape[1]
    gp = min(gp, n_pages)
    nj = n_pages // gp
    ck = jnp.transpose(cache_k, (0, 1, 3, 4, 2)).reshape(-1, FOX_W, PAGE)
    cv = jnp.transpose(cache_v, (0, 1, 3, 4, 2)).reshape(-1, FOX_W, PAGE)
    clf = jnp.transpose(cache_lf, (0, 1, 3, 2)).reshape(-1, FOX_H, PAGE)
    base = layer * n_pool
    u_mat = jnp.asarray(np.tril(np.ones((PAGE, PAGE)), -1), BF16)
    eye = jnp.eye(FOX_H, dtype=BF16)
    qbd = (eye[None, :, :, None] * qn_bf.reshape(bd, 1, FOX_H, FOX_D)).reshape(bd, FOX_H, FOX_W)
    row = lambda w: pl.BlockSpec((1, 1, w), lambda b, j, pt: (b, 0, 0))
    full = lambda *s: pl.BlockSpec(s, lambda b, j, pt: (0,) * len(s))

    def page_spec(w, g):
        return pl.BlockSpec((1, w, PAGE),
                            lambda b, j, pt: (base + pt[b, n_pages - 1 - (j * gp + g)], 0, 0))

    col = lambda w: pl.BlockSpec((1, w, 1), lambda b, j, pt: (b, 0, 0))
    in_specs = ([pl.BlockSpec((1, FOX_H, FOX_W), lambda b, j, pt: (b, 0, 0)),
                 col(FOX_H), row(FOX_W), col(FOX_W), col(FOX_W), full(PAGE, PAGE)]
                + [page_spec(FOX_W, g) for g in range(gp)]
                + [page_spec(FOX_W, g) for g in range(gp)]
                + [page_spec(FOX_H, g) for g in range(gp)])
    c3 = lambda a: a.astype(F32).reshape(bd, -1, 1)
    out = pl.pallas_call(
        functools.partial(_fox_dec_kernel, gp=gp),
        grid_spec=pltpu.PrefetchScalarGridSpec(
            num_scalar_prefetch=1,
            grid=(bd, nj),
            in_specs=in_specs,
            out_specs=col(FOX_W),
            scratch_shapes=[pltpu.VMEM((FOX_H, 1), F32), pltpu.VMEM((FOX_H, 1), F32),
                            pltpu.VMEM((FOX_W, PAGE), F32), pltpu.VMEM((FOX_H, 1), F32)]),
        out_shape=jax.ShapeDtypeStruct((bd, FOX_W, 1), F32),
        compiler_params=_cparams(("parallel", "arbitrary")),
        name="fox_decode",
    )(page_table, qbd, c3(logf8), kn.reshape(bd, 1, FOX_W), c3(v), c3(gate), u_mat,
      *([ck] * gp), *([cv] * gp), *([clf] * gp))
    return out.reshape(bd, FOX_W)


def _mla_dec_kernel(pt_ref, qa_ref, qr_ref, cnew_ref, pnew_ref, wkt_ref, *rest, gp):
    c_refs = rest[0:gp]
    p_refs = rest[gp:2 * gp]
    o_ref, m_ref, l_ref, acc_ref = rest[2 * gp:]
    j = pl.program_id(1)
    nj = pl.num_programs(1)

    @pl.when(j == 0)
    def _():
        m_ref[...] = jnp.full_like(m_ref, NEG_INF)
        l_ref[...] = jnp.zeros_like(l_ref)
        acc_ref[...] = jnp.zeros_like(acc_ref)

    qa = qa_ref[0]
    qr = qr_ref[0]

    def scores_t(cb, pt):
        n = cb.shape[0]
        kn = _dot_nt(wkt_ref[...], cb)
        ssq = jnp.sum((kn * kn).reshape(MLA_H, MLA_NOPE, n), axis=1)
        ssq = ssq + jnp.sum(pt * pt, axis=0, keepdims=True)
        raw = _dot_nt(qa, cb) + _dot(qr, pt.astype(BF16))
        return raw * lax.rsqrt(ssq * (1.0 / MLA_QK) + RMS_EPS)

    scores = []
    cbs = []
    for g in range(0, gp, 2):
        cb = jnp.concatenate([c_refs[g][0].astype(BF16), c_refs[g + 1][0].astype(BF16)], axis=0)
        pt = jnp.concatenate([p_refs[g][0], p_refs[g + 1][0]], axis=1)
        scores.append(scores_t(cb, pt))
        cbs.append(cb)
    m_prev = m_ref[...]
    m_new = m_prev
    for s in scores:
        m_new = jnp.maximum(m_new, jnp.max(s, axis=1, keepdims=True))
    alpha = jnp.exp(m_prev - m_new)
    l_new = alpha * l_ref[...]
    acc = alpha * acc_ref[...]
    for s, cb in zip(scores, cbs):
        p = jnp.exp(s - m_new)
        l_new = l_new + jnp.sum(p, axis=1, keepdims=True)
        acc = acc + _dot(p.astype(BF16), cb)
    m_ref[...] = m_new
    l_ref[...] = l_new
    acc_ref[...] = acc

    @pl.when(j == nj - 1)
    def _():
        cnew = cnew_ref[0]
        cblk = jnp.broadcast_to(cnew, (PAGE, MLA_KV_LORA)).astype(BF16)
        pblk = jnp.broadcast_to(pnew_ref[0], (MLA_ROPE, PAGE))
        s_new = scores_t(cblk, pblk)[:, 0:1]
        m_fin = jnp.maximum(m_new, s_new)
        a2 = jnp.exp(m_new - m_fin)
        p_new = jnp.exp(s_new - m_fin)
        l_fin = a2 * l_new + p_new
        num = a2 * acc + p_new * cnew
        o_ref[0] = num / l_fin


def _mla_decode(qa, qr, c_new, kpe_new, wkt_bf16, cache_c, cache_p, layer, page_table, *, gp=64):
    bd = qa.shape[0]
    n_pages = page_table.shape[1]
    n_pool = cache_c.shape[1]
    gp = min(gp, n_pages)
    nj = n_pages // gp
    cc = cache_c.reshape(-1, PAGE, MLA_KV_LORA)
    cp = jnp.transpose(cache_p, (0, 1, 3, 2)).reshape(-1, MLA_ROPE, PAGE)
    base = layer * n_pool
    full = lambda *s: pl.BlockSpec(s, lambda b, j, pt: (0,) * len(s))
    per_b = lambda *s: pl.BlockSpec((1,) + s, lambda b, j, pt: (b, 0, 0))

    def page_spec(r, c, g):
        return pl.BlockSpec((1, r, c), lambda b, j, pt: (base + pt[b, j * gp + g], 0, 0))

    in_specs = ([per_b(MLA_H, MLA_KV_LORA), per_b(MLA_H, MLA_ROPE), per_b(1, MLA_KV_LORA), per_b(MLA_ROPE, 1),
                 full(MLA_H * MLA_NOPE, MLA_KV_LORA)]
                + [page_spec(PAGE, MLA_KV_LORA, g) for g in range(gp)]
                + [page_spec(MLA_ROPE, PAGE, g) for g in range(gp)])
    lat = pl.pallas_call(
        functools.partial(_mla_dec_kernel, gp=gp),
        grid_spec=pltpu.PrefetchScalarGridSpec(
            num_scalar_prefetch=1,
            grid=(bd, nj),
            in_specs=in_specs,
            out_specs=per_b(MLA_H, MLA_KV_LORA),
            scratch_shapes=[pltpu.VMEM((MLA_H, 1), F32), pltpu.VMEM((MLA_H, 1), F32),
                            pltpu.VMEM((MLA_H, MLA_KV_LORA), F32)]),
        out_shape=jax.ShapeDtypeStruct((bd, MLA_H, MLA_KV_LORA), F32),
        compiler_params=_cparams(("parallel", "arbitrary")),
        name="mla_decode",
    )(page_table, qa, qr, c_new.reshape(bd, 1, MLA_KV_LORA), kpe_new.reshape(bd, MLA_ROPE, 1),
      wkt_bf16, *([cc] * gp), *([cp] * gp))
    return lat


def _latent_out_kernel(lat_ref, wv_ref, g_ref, o_ref):
    for h in range(MLA_H):
        o = _dot(lat_ref[:, h * MLA_KV_LORA:(h + 1) * MLA_KV_LORA].astype(BF16), wv_ref[h])
        sl = slice(h * MLA_V, (h + 1) * MLA_V)
        o_ref[:, sl] = o * _silu(g_ref[:, sl])


def _latent_out(lat, wv_bf16, gate):
    bd = lat.shape[0]
    return pl.pallas_call(
        _latent_out_kernel,
        out_shape=jax.ShapeDtypeStruct((bd, MLA_W), F32),
        compiler_params=pltpu.CompilerParams(vmem_limit_bytes=VMEM_LIMIT),
        name="latent_out",
    )(lat.reshape(bd, MLA_H * MLA_KV_LORA), wv_bf16, gate)


def _out_kernel(*refs, widths):
    n = len(widths)
    part_refs = refs[:n]
    w_ref, x_ref, o_ref = refs[n:]
    acc = x_ref[...]
    off = 0
    for p_ref, wd in zip(part_refs, widths):
        acc = acc + _dot(p_ref[...].astype(BF16), w_ref[off:off + wd, :])
        off += wd
    o_ref[...] = acc


def _out_proj(parts, w_bf16, x, *, tm=512):
    m, d = x.shape
    tm = min(tm, m)
    widths = tuple(p.shape[1] for p in parts)
    row = lambda w: pl.BlockSpec((tm, w), lambda i: (i, 0))
    return pl.pallas_call(
        functools.partial(_out_kernel, widths=widths),
        grid=(m // tm,),
        in_specs=[row(w) for w in widths] + [pl.BlockSpec(w_bf16.shape, lambda i: (0, 0)), row(d)],
        out_specs=row(d),
        out_shape=jax.ShapeDtypeStruct((m, d), F32),
        compiler_params=_cparams(("parallel",)),
        name="out_proj",
    )(*parts, w_bf16, x)


def _even_weights(w_in, b_f, q_gain, k_gain, w_g2):
    o = np.cumsum((0, FOX_W, FOX_W, FOX_W, FOX_H, FOX_W, GLA_KW, GLA_KW, GLA_W, GLA_RANK, GLA_W))
    seg = lambda i: w_in[:, o[i]:o[i + 1]]
    pad = jnp.zeros((w_in.shape[0], 128 - FOX_H - GLA_RANK), w_in.dtype)
    w = jnp.concatenate([seg(0), seg(1), seg(2), seg(4), seg(5), seg(6), seg(7), seg(9), seg(3), seg(8), pad],
                        axis=1).astype(BF16)
    bias128 = jnp.zeros((1, 128), F32).at[0, :FOX_H].set(b_f)
    qg = jnp.tile(q_gain, FOX_H).reshape(1, FOX_W)
    kg = jnp.tile(k_gain, FOX_H).reshape(1, FOX_W)
    w2pad = jnp.zeros((128, GLA_KW), F32).at[FOX_H:FOX_H + GLA_RANK].set(w_g2).astype(BF16)
    return w, bias128, qg, kg, w2pad


def _even_layer(xp, xs, e, cache_k, cache_v, cache_lf, state, page_table, norm, w_in, b_f, q_gain, k_gain,
                w_g2, b_g, out_gain, w_out):
    bp, tp, d = xp.shape
    bd = xs.shape[0]
    w, bias128, qg, kg, w2pad = _even_weights(w_in, b_f, q_gain, k_gain, w_g2)
    bg = b_g.reshape(1, GLA_KW)
    gain = out_gain.reshape(1, GLA_DV)
    w_out_bf = w_out.astype(BF16)

    x2 = xp.reshape(bp * tp, d)
    qa, ka, kn, v, vt, rest, misc = _even_in(x2, norm, w, bias128, qg, kg, tp, FOX_SCALE * LOG2E)
    r3 = lambda a: a.reshape(bp, tp, a.shape[-1])
    rest3 = r3(rest)
    fox = _flash(r3(qa), r3(ka), vt, rest3, 0, FOX_H, FOX_D, "flash_fox")
    gla, sfin = _gla_prompt(rest3, r3(misc), w2pad, bg, gain)
    xp_new = _out_proj([fox.reshape(bp * tp, FOX_W), gla.reshape(bp * tp, GLA_W)], w_out_bf, x2).reshape(bp, tp, d)
    tok_major = lambda a: a.reshape(bp, FOX_H, FOX_D, tp).transpose(0, 3, 1, 2)
    outs_p = (tok_major(kn), tok_major(v),
              r3(misc)[:, :, :FOX_H], sfin.transpose(0, 1, 3, 2))

    xs2 = xs.reshape(bd, d)
    qa, ka, kn, v, vt, rest, misc = _even_in(xs2, norm, w, bias128, qg, kg, 1, FOX_SCALE)
    qn = qa.reshape(bd, FOX_H, HP)[:, :, :FOX_D].reshape(bd, FOX_W)
    fox = _fox_decode(qn, kn, v, misc[:, :FOX_H], rest[:, :FOX_W], cache_k, cache_v, cache_lf, e, page_table)
    gla, snew = _gla_step(rest, misc, state, w_g2, b_g, gain)
    xs_new = _out_proj([fox, gla], w_out_bf, xs2).reshape(bd, 1, d)
    outs_s = (kn.reshape(bd, 1, FOX_H, FOX_D), v.reshape(bd, 1, FOX_H, FOX_D),
              misc[:, :FOX_H].reshape(bd, 1, FOX_H), snew)
    return xp_new, xs_new, outs_p, outs_s


def _odd_weights(w_in, w_q_b, w_kv_b, q_gain, k_gain):
    rows = w_in.shape[0]
    o = np.cumsum((0, MLA_Q_LORA, MLA_KV_LORA, MLA_ROPE, MLA_W))
    z = lambda n: jnp.zeros((rows, n), w_in.dtype)
    w = jnp.concatenate([w_in[:, o[0]:o[1]], w_in[:, o[1]:o[2]], z(ROPE_LO), w_in[:, o[2]:o[3]],
                         z(HP - ROPE_LO - MLA_ROPE), w_in[:, o[3]:o[4]]], axis=1).astype(BF16)
    wq = jnp.pad(w_q_b.reshape(MLA_Q_LORA, MLA_H, MLA_QK), ((0, 0), (0, 0), (0, HP - MLA_QK)))
    wq = wq.reshape(MLA_Q_LORA, MLA_H * HP).astype(BF16)
    wkv = w_kv_b.reshape(MLA_KV_LORA, MLA_H, MLA_NOPE + MLA_V)
    wk = wkv[:, :, :MLA_NOPE]
    wk_pad = jnp.pad(wk, ((0, 0), (0, 0), (0, HP - MLA_NOPE))).reshape(MLA_KV_LORA, MLA_H * HP).astype(BF16)
    wv = wkv[:, :, MLA_NOPE:]
    padg = lambda g: jnp.pad(g, (0, HP - MLA_QK)).reshape(1, HP)
    return w, wq, wk, wk_pad, wv, padg(q_gain), padg(k_gain)


def _rope_tiles(pos):
    inv = ROPE_THETA ** (-jnp.arange(ROPE_HALF, dtype=F32) / ROPE_HALF)
    ang = pos.astype(F32)[:, None] * inv[None, :]
    cos, sin = jnp.cos(ang), jnp.sin(ang)
    n = pos.shape[0]
    one = jnp.ones((n, ROPE_LO), F32)
    z = lambda w: jnp.zeros((n, w), F32)
    tail = HP - ROPE_LO - MLA_ROPE
    cos_t = jnp.concatenate([one, cos, cos, z(tail)], axis=1)
    sa_t = jnp.concatenate([z(ROPE_LO + ROPE_HALF), sin, z(tail)], axis=1)
    sb_t = jnp.concatenate([z(ROPE_LO), -sin, z(ROPE_HALF + tail)], axis=1)
    return cos_t, sa_t, sb_t


def _odd_layer(xp, xs, o, cache_c, cache_p, page_table, norm, w_in, q_a_gain, w_q_b, kv_a_gain, w_kv_b,
               q_gain, k_gain, w_out):
    bp, tp, d = xp.shape
    bd = xs.shape[0]
    past = page_table.shape[1] * PAGE
    w, wq, wk, wk_pad, wv, qg, kg = _odd_weights(w_in, w_q_b, w_kv_b, q_gain, k_gain)
    wv_flat = wv.reshape(MLA_KV_LORA, MLA_W).astype(BF16)
    w_out_bf = w_out.astype(BF16)

    x2 = xp.reshape(bp * tp, d)
    q, k, vt, c, kpe, gate = _odd_in(x2, norm, w, q_a_gain, wq, kv_a_gain, wk_pad, wv_flat, qg, kg,
                                     _rope_tiles(jnp.arange(tp)), MLA_SCALE * LOG2E)
    r3 = lambda a: a.reshape(bp, tp, a.shape[-1])
    att = _flash(r3(q), r3(k), vt, r3(gate), 0, MLA_H, MLA_V, "flash_mla")
    xp_new = _out_proj([att.reshape(bp * tp, MLA_W)], w_out_bf, x2).reshape(bp, tp, d)
    outs_p = (c.reshape(bp, tp, MLA_KV_LORA), kpe.reshape(bp, tp, MLA_ROPE))

    xs2 = xs.reshape(bd, d)
    q, _, _, c, kpe, gate = _odd_in(xs2, norm, w, q_a_gain, wq, kv_a_gain, wk_pad, wv_flat, qg, kg,
                                    _rope_tiles(jnp.full((bd,), past, jnp.int32)), MLA_SCALE)
    qk = q.astype(F32).reshape(bd, MLA_H, HP)[:, :, :MLA_QK] * k_gain
    qa = jnp.einsum('bhd,rhd->bhr', qk[:, :, :MLA_NOPE], wk).astype(BF16)
    qr = qk[:, :, MLA_NOPE:].astype(BF16)
    wkt = wk.reshape(MLA_KV_LORA, MLA_H * MLA_NOPE).T.astype(BF16)
    lat = _mla_decode(qa, qr, c, kpe, wkt, cache_c, cache_p, o, page_table)
    att = _latent_out(lat, wv.transpose(1, 0, 2).astype(BF16), gate)
    xs_new = _out_proj([att], w_out_bf, xs2).reshape(bd, 1, d)
    outs_s = (c.reshape(bd, 1, MLA_KV_LORA), kpe.reshape(bd, 1, MLA_ROPE))
    return xp_new, xs_new, outs_p, outs_s


def kernel(x_prompt, x_sample, cache_fox_k, cache_fox_v, cache_fox_logf, cache_mla_ckv, cache_mla_kpe,
           state_gla, page_table, norm_even, w_in_even, b_fox_f, fox_q_gain, fox_k_gain, gla_w_gate2,
           gla_b_gate, gla_out_gain, w_out_even, norm_odd, w_in_odd, mla_q_a_gain, w_q_b, mla_kv_a_gain,
           w_kv_b, mla_q_gain, mla_k_gain, w_out_odd):
    depth = norm_even.shape[0] + norm_odd.shape[0]
    xp, xs = x_prompt, x_sample
    ev_p, ev_s, od_p, od_s = [], [], [], []
    for layer in range(depth):
        i = layer // 2
        if layer % 2 == 0:
            xp, xs, op, os_ = _even_layer(
                xp, xs, i, cache_fox_k, cache_fox_v, cache_fox_logf, state_gla[i], page_table,
                norm_even[i], w_in_even[i], b_fox_f[i], fox_q_gain[i], fox_k_gain[i], gla_w_gate2[i],
                gla_b_gate[i], gla_out_gain[i], w_out_even[i])
            ev_p.append(op)
            ev_s.append(os_)
        else:
            xp, xs, op, os_ = _odd_layer(
                xp, xs, i, cache_mla_ckv, cache_mla_kpe, page_table, norm_odd[i], w_in_odd[i],
                mla_q_a_gain[i], w_q_b[i], mla_kv_a_gain[i], w_kv_b[i], mla_q_gain[i], mla_k_gain[i],
                w_out_odd[i])
            od_p.append(op)
            od_s.append(os_)
    st = lambda lst, k: jnp.stack([t[k] for t in lst])
    return (xp, xs,
            st(ev_p, 0), st(ev_p, 1), st(ev_p, 2), st(od_p, 0), st(od_p, 1), st(ev_p, 3),
            st(ev_s, 0), st(ev_s, 1), st(ev_s, 2), st(od_s, 0), st(od_s, 1), st(ev_s, 3))
```

```python
import functools

import numpy as np
import jax
import jax.numpy as jnp
from jax import lax
from jax.experimental import pallas as pl
from jax.experimental.pallas import tpu as pltpu

F32 = jnp.float32
BF16 = jnp.bfloat16

D_MODEL = 1024
PAGE = 128
RMS_EPS = 1e-6
NEG_INF = -1e30
FOX_H = 8
FOX_D = 64
FOX_W = FOX_H * FOX_D
FOX_SCALE = FOX_D ** -0.5
GLA_H = 4
GLA_DK = 64
GLA_DV = 128
GLA_KW = GLA_H * GLA_DK
GLA_W = GLA_H * GLA_DV
GLA_RANK = 16
GLA_GATE_NORM = 16.0
GLA_CHUNK = 64
MLA_H = 16
MLA_Q_LORA = 256
MLA_KV_LORA = 128
MLA_NOPE = 64
MLA_ROPE = 32
MLA_QK = MLA_NOPE + MLA_ROPE
MLA_V = 64
MLA_W = MLA_H * MLA_V
MLA_SCALE = MLA_QK ** -0.5
HP = 128
ROPE_THETA = 10000.0
LOG2E = 1.4426950408889634

VMEM_LIMIT = 56 * 1024 * 1024

E_Q, E_K, E_V, E_REST, E_MISC, E_END = 0, 512, 1024, 1536, 3584, 3712


def _cparams(sem):
    return pltpu.CompilerParams(dimension_semantics=sem, vmem_limit_bytes=VMEM_LIMIT)


def _log_sigmoid(x):
    return jnp.minimum(x, 0.0) - jnp.log1p(jnp.exp(-jnp.abs(x)))


def _silu(x):
    return x / (1.0 + jnp.exp(-x))


def _split_bf16(x):
    hi = x.astype(BF16)
    lo = (x - hi.astype(F32)).astype(BF16)
    return hi, lo


def _dot(a, b):
    return jnp.dot(a, b, preferred_element_type=F32)


def _dot_nt(a, b):
    return lax.dot_general(a, b, (((1,), (1,)), ((), ())), preferred_element_type=F32)


def _split3_bf16(x):
    hi = x.astype(BF16)
    r = x - hi.astype(F32)
    mid = r.astype(BF16)
    lo = (r - mid.astype(F32)).astype(BF16)
    return hi, mid, lo


def _rmsnorm(x, gain):
    return x * lax.rsqrt(jnp.mean(x * x, axis=-1, keepdims=True) + RMS_EPS) * gain


O_QA, O_KVA, O_KPE, O_G, O_END = 0, 256, 384, 512, 1536
ROPE_LO = MLA_NOPE
ROPE_HALF = MLA_ROPE // 2


def _odd_in_kernel(x_ref, g_ref, w_ref, qag_ref, wq_ref, kvg_ref, wk_ref, wv_ref, qg_ref, kg_ref, ones_ref,
                   cos_ref, sa_ref, sb_ref, q_ref, k_ref, vt_ref, c_ref, kpe_ref, gate_ref, *, q_scale):
    xn = _rmsnorm(x_ref[...], g_ref[...]).astype(BF16)

    def mm(lo, hi):
        return _dot(xn, w_ref[:, lo:hi])

    cos, sa, sb = cos_ref[...], sa_ref[...], sb_ref[...]

    def rope(t):
        return (t * cos + pltpu.roll(t, ROPE_HALF, 1) * sa + pltpu.roll(t, HP - ROPE_HALF, 1) * sb)

    def headnorm(t, gain):
        ss = _dot((t * t).astype(BF16), ones_ref[...])
        return t * lax.rsqrt(ss * (1.0 / MLA_QK) + RMS_EPS) * gain

    gate_ref[...] = mm(O_G, O_END)
    c = _rmsnorm(mm(O_KVA, O_KPE), kvg_ref[...])
    c_ref[...] = c
    kpe_t = rope(mm(O_KPE, O_G))
    kpe_ref[...] = kpe_t[:, ROPE_LO:ROPE_LO + MLA_ROPE]
    qa = _rmsnorm(mm(O_QA, O_KVA), qag_ref[...]).astype(BF16)
    cb = c.astype(BF16)
    for h in range(MLA_H):
        sl = slice(h * HP, (h + 1) * HP)
        qt = rope(_dot(qa, wq_ref[:, sl]))
        q_ref[:, sl] = (headnorm(qt, qg_ref[...]) * q_scale).astype(BF16)
        kt = _dot(cb, wk_ref[:, sl]) + kpe_t
        k_ref[:, sl] = headnorm(kt, kg_ref[...]).astype(BF16)
    vt_ref[...] = _dot(cb, wv_ref[...]).T.astype(BF16)


def _odd_in(x, gain, w, qag, wq, kvg, wk, wv, qg, kg, tables, q_scale, *, tm=256):
    m, d = x.shape
    tm = min(tm, m)
    cos_t, sa_t, sb_t = tables
    n_tab = cos_t.shape[0] // tm
    row = lambda wd: pl.BlockSpec((tm, wd), lambda i: (i, 0))
    full = lambda a, b: pl.BlockSpec((a, b), lambda i: (0, 0))
    tab = pl.BlockSpec((tm, HP), lambda i: (i % n_tab, 0))
    ones = jnp.ones((HP, HP), BF16)
    return pl.pallas_call(
        functools.partial(_odd_in_kernel, q_scale=q_scale),
        grid=(m // tm,),
        in_specs=[row(d), full(1, d), full(d, O_END), full(1, MLA_Q_LORA), full(MLA_Q_LORA, MLA_H * HP),
                  full(1, MLA_KV_LORA), full(MLA_KV_LORA, MLA_H * HP), full(MLA_KV_LORA, MLA_W),
                  full(1, HP), full(1, HP), full(HP, HP), tab, tab, tab],
        out_specs=[row(MLA_H * HP), row(MLA_H * HP), pl.BlockSpec((MLA_W, tm), lambda i: (0, i)),
                   row(MLA_KV_LORA), row(MLA_ROPE), row(MLA_W)],
        out_shape=[jax.ShapeDtypeStruct((m, MLA_H * HP), BF16),
                   jax.ShapeDtypeStruct((m, MLA_H * HP), BF16),
                   jax.ShapeDtypeStruct((MLA_W, m), BF16),
                   jax.ShapeDtypeStruct((m, MLA_KV_LORA), F32),
                   jax.ShapeDtypeStruct((m, MLA_ROPE), F32),
                   jax.ShapeDtypeStruct((m, MLA_W), F32)],
        compiler_params=_cparams(("parallel",)),
        name="odd_in_proj",
    )(x, gain.reshape(1, d), w, qag.reshape(1, -1), wq, kvg.reshape(1, -1), wk, wv, qg, kg, ones,
      cos_t, sa_t, sb_t)


AUG_Q = FOX_D
AUG_K = FOX_D + 3


def _even_in_kernel(x_ref, g_ref, w_ref, bias_ref, qg_ref, kg_ref, ones_ref, tri_ref, place_ref, pc_ref, aug_ref,
                    qa_ref, ka_ref, kn_ref, v_ref, vt_ref, rest_ref, misc_ref,
                    carry_ref, *, tiles_per_seq, do_cum, q_scale):
    xn = _rmsnorm(x_ref[...], g_ref[...]).astype(BF16)

    def mm(lo, hi):
        return _dot(xn, w_ref[:, lo:hi])

    def headnorm(z, gain):
        ss = _dot((z * z).astype(BF16), ones_ref[...])
        return z * lax.rsqrt(ss * (1.0 / FOX_D) + RMS_EPS) * gain

    q = headnorm(mm(E_Q, E_K), qg_ref[...]) * q_scale
    k = headnorm(mm(E_K, E_V), kg_ref[...])
    v = mm(E_V, E_REST)
    vt = v.T
    vt_ref[...] = vt.astype(BF16)
    if do_cum:
        kn_ref[0] = k.T
        v_ref[0] = vt
    else:
        kn_ref[...] = k
        v_ref[...] = v
    rest_ref[...] = mm(E_REST, E_MISC)
    mz = mm(E_MISC, E_END) + bias_ref[...]
    lane = lax.broadcasted_iota(jnp.int32, mz.shape, 1)
    logf = jnp.where(lane < FOX_H, _log_sigmoid(mz), mz)
    misc_ref[...] = logf
    qa = _dot(q.astype(BF16), place_ref[...])
    ka = _dot(k.astype(BF16), place_ref[...])
    if do_cum:
        @pl.when(pl.program_id(0) % tiles_per_seq == 0)
        def _():
            carry_ref[...] = jnp.zeros_like(carry_ref)

        lf = jnp.where(lane < FOX_H, logf, 0.0)
        hi, lo = _split_bf16(lf)
        c = _dot(tri_ref[...], hi) + _dot(tri_ref[...], lo) + carry_ref[...]
        carry_ref[...] = c[c.shape[0] - 1:, :]
        pieces = _split3_bf16(c * LOG2E)
        for p in range(3):
            qa = qa + _dot(pieces[p], pc_ref[p])
            ka = ka + _dot(pieces[p], pc_ref[3 + p])
        qa = qa + aug_ref[0:1, :]
        ka = ka + aug_ref[1:2, :]
    qa_ref[...] = qa.astype(BF16)
    ka_ref[...] = ka.astype(BF16)


def _even_in(x, gain, w_bf16, bias128, qg, kg, seq_len, q_scale, *, tm=256):
    m, k = x.shape
    tm = min(tm, m, seq_len) if seq_len > 1 else min(tm, m)
    do_cum = seq_len > 1
    tiles_per_seq = max(seq_len // tm, 1)
    ones_bd = jnp.asarray(np.kron(np.eye(FOX_H), np.ones((FOX_D, FOX_D))), BF16)
    tri = jnp.asarray(np.tril(np.ones((tm, tm))), BF16)
    place = np.zeros((FOX_H, FOX_D, FOX_H, HP), np.float32)
    pc = np.zeros((6, 128, FOX_H, HP), np.float32)
    aug = np.zeros((2, FOX_H, HP), np.float32)
    for h in range(FOX_H):
        place[h, np.arange(FOX_D), h, np.arange(FOX_D)] = 1.0
        for p in range(3):
            pc[p, h, h, AUG_Q + p] = 1.0
            pc[3 + p, h, h, AUG_K + p] = -1.0
        aug[0, h, AUG_K:AUG_K + 3] = 1.0
        aug[1, h, AUG_Q:AUG_Q + 3] = 1.0
    place = jnp.asarray(place.reshape(FOX_W, FOX_H * HP), BF16)
    pc = jnp.asarray(pc.reshape(6, 128, FOX_H * HP), BF16)
    aug = jnp.asarray(aug.reshape(2, FOX_H * HP), F32)
    row = lambda w: pl.BlockSpec((tm, w), lambda i: (i, 0))
    full = lambda *s: pl.BlockSpec(s, lambda i: (0,) * len(s))
    if do_cum:
        kv_spec = pl.BlockSpec((1, FOX_W, tm), lambda i: (i // tiles_per_seq, 0, i % tiles_per_seq))
        kv_shape = jax.ShapeDtypeStruct((m // seq_len, FOX_W, seq_len), F32)
    else:
        kv_spec = row(FOX_W)
        kv_shape = jax.ShapeDtypeStruct((m, FOX_W), F32)
    outs = pl.pallas_call(
        functools.partial(_even_in_kernel, tiles_per_seq=tiles_per_seq, do_cum=do_cum, q_scale=q_scale),
        grid=(m // tm,),
        in_specs=[row(k), full(1, k), full(k, E_END), full(1, 128), full(1, FOX_W), full(1, FOX_W),
                  full(FOX_W, FOX_W), full(tm, tm), full(FOX_W, FOX_H * HP), full(6, 128, FOX_H * HP),
                  full(2, FOX_H * HP)],
        out_specs=[row(FOX_H * HP), row(FOX_H * HP), kv_spec, kv_spec,
                   pl.BlockSpec((FOX_W, tm), lambda i: (0, i)),
                   row(E_MISC - E_REST), row(128)],
        out_shape=[jax.ShapeDtypeStruct((m, FOX_H * HP), BF16),
                   jax.ShapeDtypeStruct((m, FOX_H * HP), BF16),
                   kv_shape, kv_shape,
                   jax.ShapeDtypeStruct((FOX_W, m), BF16),
                   jax.ShapeDtypeStruct((m, E_MISC - E_REST), F32),
                   jax.ShapeDtypeStruct((m, 128), F32)],
        scratch_shapes=[pltpu.VMEM((1, 128), F32)],
        compiler_params=_cparams(("arbitrary",)),
        name="even_in_proj",
    )(x, gain.reshape(1, k), w_bf16, bias128, qg, kg, ones_bd, tri, place, pc, aug)
    return outs


ONES_ROWS = 16


def _flash_kernel(qi_ref, kj_ref, q_ref, k_ref, vt_ref, g_ref, o_ref, m_ref, acc_ref, *, n_heads, dv):
    i = qi_ref[pl.program_id(1)]
    j = kj_ref[pl.program_id(1)]

    @pl.when(j == 0)
    def _():
        m_ref[...] = jnp.full_like(m_ref, NEG_INF)
        acc_ref[...] = jnp.zeros_like(acc_ref)

    def step(masked):
        tq = q_ref.shape[1]
        tk = k_ref.shape[1]
        ones = jnp.ones((ONES_ROWS, tk), BF16)
        if masked:
            r = lax.broadcasted_iota(jnp.int32, (tk, tq), 0)
            c = lax.broadcasted_iota(jnp.int32, (tk, tq), 1)
            keep = c >= r
        for h in range(n_heads):
            q = q_ref[0, :, h * HP:(h + 1) * HP]
            k = k_ref[0, :, h * HP:(h + 1) * HP]
            s = _dot_nt(k, q)
            if masked:
                s = jnp.where(keep, s, NEG_INF)
            m_prev = m_ref[h]
            m_new = jnp.maximum(m_prev, jnp.max(s, axis=0, keepdims=True))
            alpha = jnp.exp2(m_prev - m_new)
            p = jnp.exp2(s - m_new)
            va = jnp.concatenate([vt_ref[h * dv:(h + 1) * dv, :], ones], axis=0)
            acc_ref[h] = alpha * acc_ref[h] + _dot(va, p.astype(BF16))
            m_ref[h] = m_new

    @pl.when(j < i)
    def _():
        step(False)

    @pl.when(j == i)
    def _():
        step(True)
        for h in range(0, n_heads, 2):
            sl = slice(h * dv, (h + 2) * dv)
            a0, a1 = acc_ref[h], acc_ref[h + 1]
            o = jnp.concatenate([a0[:dv] / a0[dv:dv + 1], a1[:dv] / a1[dv:dv + 1]], axis=0)
            o_ref[0, :, sl] = (o.T * _silu(g_ref[0, :, sl])).astype(o_ref.dtype)


def _flash(q, k, vt, gate, gate_block, n_heads, dv, name, *, t_blk=512):
    b, t, _ = q.shape
    tb = min(t_blk, t)
    n = t // tb
    pairs = [(i, j) for i in range(n) for j in range(i + 1)]
    qi = jnp.asarray([p[0] for p in pairs], jnp.int32)
    kj = jnp.asarray([p[1] for p in pairs], jnp.int32)
    qmap = lambda bi, s, qi, kj: (bi, qi[s], 0)
    kmap = lambda bi, s, qi, kj: (bi, kj[s], 0)
    in_specs = [pl.BlockSpec((1, tb, n_heads * HP), qmap),
                pl.BlockSpec((1, tb, n_heads * HP), kmap),
                pl.BlockSpec((n_heads * dv, tb), lambda bi, s, qi, kj: (0, bi * n + kj[s])),
                pl.BlockSpec((1, tb, n_heads * dv), lambda bi, s, qi, kj: (bi, qi[s], gate_block))]
    return pl.pallas_call(
        functools.partial(_flash_kernel, n_heads=n_heads, dv=dv),
        grid_spec=pltpu.PrefetchScalarGridSpec(
            num_scalar_prefetch=2,
            grid=(b, len(pairs)),
            in_specs=in_specs,
            out_specs=pl.BlockSpec((1, tb, n_heads * dv), qmap),
            scratch_shapes=[pltpu.VMEM((n_heads, 1, tb), F32),
                            pltpu.VMEM((n_heads, dv + ONES_ROWS, tb), F32)]),
        out_shape=jax.ShapeDtypeStruct((b, t, n_heads * dv), BF16),
        compiler_params=_cparams(("parallel", "arbitrary")),
        name=name,
    )(qi, kj, q, k, vt, gate)


PAIR_BLK = 16


def _gla_gate(misc, w2_ref, bg_ref):
    pre = _dot(misc.astype(BF16), w2_ref[...]) + bg_ref[...]
    return _log_sigmoid(pre) * (1.0 / GLA_GATE_NORM)


def _gla_kernel(q_ref, k_ref, v_ref, gg_ref, misc_ref, w2_ref, bg_ref, gain_ref, hsel_ref, tri_ref,
                o_ref, sfin_ref, st_ref, w_s, *, cps):
    c = pl.program_id(1)
    n_c = pl.num_programs(1)

    @pl.when(c == 0)
    def _():
        st_ref[...] = jnp.zeros_like(st_ref)

    for ci in range(cps):
        rs = pl.ds(ci * GLA_CHUNK, GLA_CHUNK)
        _gla_chunk(q_ref.at[0, rs], k_ref.at[0, rs], v_ref.at[0, rs], gg_ref.at[0, rs], misc_ref.at[0, rs],
                   w2_ref, bg_ref, gain_ref, hsel_ref, tri_ref, o_ref.at[0, rs], st_ref, w_s.at[ci])

    @pl.when(c == n_c - 1)
    def _():
        sfin_ref[0] = st_ref[...]


def _gla_chunk(q_ref, k_ref, v_ref, gg_ref, misc_ref, w2_ref, bg_ref, gain_ref, hsel_ref, tri_ref,
               o_ref, st_ref, w_s):
    ch = GLA_CHUNK
    q = q_ref[...] * (GLA_DK ** -0.5)
    k = k_ref[...]
    v = v_ref[...]
    la = _gla_gate(misc_ref[...], w2_ref, bg_ref)
    hi, lo = _split_bf16(la)
    cum = _dot(tri_ref[...], hi) + _dot(tri_ref[...], lo)
    nb = ch // PAIR_BLK
    blk = lambda a, i: a[i * PAIR_BLK:(i + 1) * PAIR_BLK]

    for j in range(ch):
        ib = j // PAIR_BLK
        w = (jnp.exp(jnp.minimum(blk(cum, ib) - cum[j:j + 1, :], 0.0)) * (blk(q, ib) * k[j:j + 1, :])).astype(BF16)
        off = (j - ib * PAIR_BLK) * 128
        w_s[0, ib * PAIR_BLK:(ib + 1) * PAIR_BLK, off:off + 128] = w[:, 0:128]
        w_s[1, ib * PAIR_BLK:(ib + 1) * PAIR_BLK, off:off + 128] = w[:, 128:256]
    r = lax.broadcasted_iota(jnp.int32, (PAIR_BLK, PAIR_BLK), 0)
    cc = lax.broadcasted_iota(jnp.int32, (PAIR_BLK, PAIR_BLK), 1)
    causal = r >= cc
    diag = [[None] * nb for _ in range(GLA_H)]
    for lt in range(2):
        for ib in range(nb):
            a2 = _dot_nt(w_s[lt, ib * PAIR_BLK:(ib + 1) * PAIR_BLK, :], hsel_ref[...])
            for hh in range(2):
                diag[2 * lt + hh][ib] = jnp.where(causal, a2[:, hh * PAIR_BLK:(hh + 1) * PAIR_BLK], 0.0)
    qes, kes = [None], [None]
    for ib in range(1, nb):
        ref = cum[ib * PAIR_BLK - 1:ib * PAIR_BLK, :]
        qes.append((blk(q, ib) * jnp.exp(blk(cum, ib) - ref)).astype(BF16))
        kes.append((k[:ib * PAIR_BLK] * jnp.exp(ref - cum[:ib * PAIR_BLK])).astype(BF16))

    for lt in range(2):
        for hh in range(2):
            h = 2 * lt + hh
            sk = slice(h * GLA_DK, (h + 1) * GLA_DK)
            sv = slice(h * GLA_DV, (h + 1) * GLA_DV)
            vh = v[:, sv]
            vb = vh.astype(BF16)
            rows = []
            for ib in range(nb):
                o_b = _dot(diag[h][ib].astype(BF16), blk(vb, ib))
                if ib > 0:
                    a_off = _dot_nt(qes[ib][:, sk], kes[ib][:, sk])
                    o_b = o_b + _dot(a_off.astype(BF16), vb[:ib * PAIR_BLK])
                rows.append(o_b)
            o_intra = jnp.concatenate(rows, axis=0)
            cum_h = cum[:, sk]
            last = cum_h[ch - 1:, :]
            st = st_ref[h]
            qe = (q[:, sk] * jnp.exp(cum_h)).astype(BF16)
            o = _dot_nt(qe, st.astype(BF16)) + o_intra
            kd = (k[:, sk] * jnp.exp(last - cum_h)).astype(BF16)
            st_ref[h] = st * jnp.exp(last) + _dot(vh.T.astype(BF16), kd)
            ms = jnp.mean(o * o, axis=-1, keepdims=True)
            y = o * lax.rsqrt(ms + RMS_EPS) * gain_ref[...] * _silu(gg_ref[:, sv])
            o_ref[:, sv] = y.astype(o_ref.dtype)


def _gla_prompt(rest, misc, w2pad, bg, gain):
    b, t, _ = rest.shape
    ch = GLA_CHUNK
    n = t // ch
    hsel = np.zeros((2, PAIR_BLK, PAIR_BLK, 2, GLA_DK), np.float32)
    for hh in range(2):
        for j in range(PAIR_BLK):
            hsel[hh, j, j, hh, :] = 1.0
    hsel = jnp.asarray(hsel.reshape(2 * PAIR_BLK, PAIR_BLK * 128), BF16)
    tri = jnp.asarray(np.tril(np.ones((ch, ch))), BF16)
    full = lambda *s: pl.BlockSpec(s, lambda bi, ci: (0,) * len(s))
    cps = 2 if n % 2 == 0 else 1
    rows = cps * ch
    o, sfin = pl.pallas_call(
        functools.partial(_gla_kernel, cps=cps),
        grid=(b, n // cps),
        in_specs=[pl.BlockSpec((1, rows, GLA_KW), lambda bi, ci: (bi, ci, 2)),
                  pl.BlockSpec((1, rows, GLA_KW), lambda bi, ci: (bi, ci, 3)),
                  pl.BlockSpec((1, rows, GLA_W), lambda bi, ci: (bi, ci, 2)),
                  pl.BlockSpec((1, rows, GLA_W), lambda bi, ci: (bi, ci, 3)),
                  pl.BlockSpec((1, rows, 128), lambda bi, ci: (bi, ci, 0)),
                  full(128, GLA_KW), full(1, GLA_KW), full(1, GLA_DV), full(2 * PAIR_BLK, PAIR_BLK * 128),
                  full(ch, ch)],
        out_specs=[pl.BlockSpec((1, rows, GLA_W), lambda bi, ci: (bi, ci, 0)),
                   pl.BlockSpec((1, GLA_H, GLA_DV, GLA_DK), lambda bi, ci: (bi, 0, 0, 0))],
        out_shape=[jax.ShapeDtypeStruct((b, t, GLA_W), BF16),
                   jax.ShapeDtypeStruct((b, GLA_H, GLA_DV, GLA_DK), F32)],
        scratch_shapes=[pltpu.VMEM((GLA_H, GLA_DV, GLA_DK), F32),
                        pltpu.VMEM((cps, 2, ch, PAIR_BLK * 128), BF16)],
        compiler_params=_cparams(("parallel", "arbitrary")),
        name="gla_prompt",
    )(rest, rest, rest, rest, misc, w2pad, bg, gain, hsel, tri)
    return o, sfin


def _gla_step_kernel(qT_ref, kT_ref, gaT_ref, v_ref, gg_ref, s_ref, w2T_ref, bgT_ref, gain_ref,
                     o_ref, snew_ref, *, tb):
    laT = _log_sigmoid(_dot(w2T_ref[...], gaT_ref[0].astype(BF16)) + bgT_ref[...]) * (1.0 / GLA_GATE_NORM)
    eT = jnp.exp(laT)
    for bb in range(tb):
        for h in range(GLA_H):
            sk = slice(h * GLA_DK, (h + 1) * GLA_DK)
            sv = slice(h * GLA_DV, (h + 1) * GLA_DV)
            s_new = s_ref[bb, h] * eT[sk, bb:bb + 1] + kT_ref[0, sk, bb:bb + 1] * v_ref[bb, :, sv]
            snew_ref[bb, h] = s_new
            qcol = qT_ref[0, sk, bb:bb + 1] * (GLA_DK ** -0.5)
            o = jnp.sum(qcol * s_new, axis=0, keepdims=True)
            ms = jnp.mean(o * o, axis=-1, keepdims=True)
            y = o * lax.rsqrt(ms + RMS_EPS) * gain_ref[...] * _silu(gg_ref[bb, :, sv])
            o_ref[bb, :, sv] = y


def _gla_step(rest, misc, state, w2, bg, gain, *, tb=8):
    bd = rest.shape[0]
    tb = min(tb, bd)
    g = bd // tb
    tr = lambda a: a.reshape(g, tb, a.shape[-1]).transpose(0, 2, 1)
    qT = tr(rest[:, 512:768])
    kT = tr(rest[:, 768:1024])
    gaT = tr(misc[:, 8:8 + GLA_RANK])
    v3 = rest[:, 1024:1536].reshape(bd, 1, GLA_W)
    gg3 = rest[:, 1536:2048].reshape(bd, 1, GLA_W)
    blk = lambda *s: pl.BlockSpec(s, lambda i: (i,) + (0,) * (len(s) - 1))
    full = lambda *s: pl.BlockSpec(s, lambda i: (0,) * len(s))
    o, snew = pl.pallas_call(
        functools.partial(_gla_step_kernel, tb=tb),
        grid=(g,),
        in_specs=[blk(1, GLA_KW, tb), blk(1, GLA_KW, tb), blk(1, GLA_RANK, tb),
                  blk(tb, 1, GLA_W), blk(tb, 1, GLA_W), blk(tb, GLA_H, GLA_DK, GLA_DV),
                  full(GLA_KW, GLA_RANK), full(GLA_KW, 1), full(1, GLA_DV)],
        out_specs=[blk(tb, 1, GLA_W), blk(tb, GLA_H, GLA_DK, GLA_DV)],
        out_shape=[jax.ShapeDtypeStruct((bd, 1, GLA_W), F32),
                   jax.ShapeDtypeStruct((bd, GLA_H, GLA_DK, GLA_DV), F32)],
        compiler_params=_cparams(("parallel",)),
        name="gla_step",
    )(qT, kT, gaT, v3, gg3, state, w2.T.astype(BF16), bg.reshape(GLA_KW, 1), gain)
    return o.reshape(bd, GLA_W), snew


def _head_rows(a, n_heads, width):
    return jnp.concatenate([jnp.broadcast_to(a[h:h + 1, :], (width, 1)) for h in range(n_heads)], axis=0)


def _fox_dec_kernel(pt_ref, qbd_ref, cnew_ref, knew_ref, vnew_ref, g_ref, u_ref, *rest, gp):
    k_refs = rest[0:gp]
    v_refs = rest[gp:2 * gp]
    lf_refs = rest[2 * gp:3 * gp]
    o_ref, m_ref, l_ref, acc_ref, sfx_ref = rest[3 * gp:]
    j = pl.program_id(1)
    nj = pl.num_programs(1)

    @pl.when(j == 0)
    def _():
        m_ref[...] = jnp.full_like(m_ref, NEG_INF)
        l_ref[...] = jnp.zeros_like(l_ref)
        acc_ref[...] = jnp.zeros_like(acc_ref)
        sfx_ref[...] = jnp.zeros_like(sfx_ref)

    qbd = qbd_ref[0]
    cnew = cnew_ref[0]
    carry = sfx_ref[...]
    scores = []
    for g in range(gp):
        lf = lf_refs[g][0]
        hi, lo = _split_bf16(lf)
        sfx = _dot(hi, u_ref[...]) + _dot(lo, u_ref[...]) + carry
        carry = carry + jnp.sum(lf, axis=1, keepdims=True)
        s = _dot(qbd, k_refs[g][0].astype(BF16)) + sfx + cnew
        scores.append(s)
    sfx_ref[...] = carry
    m_prev = m_ref[...]
    m_new = m_prev
    for s in scores:
        m_new = jnp.maximum(m_new, jnp.max(s, axis=1, keepdims=True))
    alpha = jnp.exp(m_prev - m_new)
    ps = [jnp.exp(s - m_new) for s in scores]
    l_new = alpha * l_ref[...]
    for p in ps:
        l_new = l_new + jnp.sum(p, axis=1, keepdims=True)
    for h in range(FOX_H):
        rows = slice(h * FOX_D, (h + 1) * FOX_D)
        a = acc_ref[rows, :] * alpha[h:h + 1, :]
        for g in range(gp):
            a = a + ps[g][h:h + 1, :] * v_refs[g][0, rows, :]
        acc_ref[rows, :] = a
    m_ref[...] = m_new
    l_ref[...] = l_new

    @pl.when(j == nj - 1)
    def _():
        s_new = jnp.sum(qbd.astype(F32) * knew_ref[0], axis=1, keepdims=True)
        m_fin = jnp.maximum(m_new, s_new)
        a2 = jnp.exp(m_new - m_fin)
        p_new = jnp.exp(s_new - m_fin)
        l_fin = a2 * l_new + p_new
        tot = jnp.sum(acc_ref[...], axis=1, keepdims=True)
        num = _head_rows(a2, FOX_H, FOX_D) * tot + _head_rows(p_new, FOX_H, FOX_D) * vnew_ref[0]
        o = num / _head_rows(l_fin, FOX_H, FOX_D)
        o_ref[0] = o * _silu(g_ref[0])


def _fox_decode(qn_bf, kn, v, logf8, gate, cache_k, cache_v, cache_lf, layer, page_table, *, gp=32):
    bd = qn_bf.shape[0]
    n_pages = page_table.shape[1]
    n_pool = cache_k.shape[1]
    gp = min(gp, n_pages)
    nj = n_pages // gp
    ck = jnp.transpose(cache_k, (0, 1, 3, 4, 2)).reshape(-1, FOX_W, PAGE)
    cv = jnp.transpose(cache_v, (0, 1, 3, 4, 2)).reshape(-1, FOX_W, PAGE)
    clf = jnp.transpose(cache_lf, (0, 1, 3, 2)).reshape(-1, FOX_H, PAGE)
    base = layer * n_pool
    u_mat = jnp.asarray(np.tril(np.ones((PAGE, PAGE)), -1), BF16)
    eye = jnp.eye(FOX_H, dtype=BF16)
    qbd = (eye[None, :, :, None] * qn_bf.reshape(bd, 1, FOX_H, FOX_D)).reshape(bd, FOX_H, FOX_W)
    row = lambda w: pl.BlockSpec((1, 1, w), lambda b, j, pt: (b, 0, 0))
    full = lambda *s: pl.BlockSpec(s, lambda b, j, pt: (0,) * len(s))

    def page_spec(w, g):
        return pl.BlockSpec((1, w, PAGE),
                            lambda b, j, pt: (base + pt[b, n_pages - 1 - (j * gp + g)], 0, 0))

    col = lambda w: pl.BlockSpec((1, w, 1), lambda b, j, pt: (b, 0, 0))
    in_specs = ([pl.BlockSpec((1, FOX_H, FOX_W), lambda b, j, pt: (b, 0, 0)),
                 col(FOX_H), row(FOX_W), col(FOX_W), col(FOX_W), full(PAGE, PAGE)]
                + [page_spec(FOX_W, g) for g in range(gp)]
                + [page_spec(FOX_W, g) for g in range(gp)]
                + [page_spec(FOX_H, g) for g in range(gp)])
    c3 = lambda a: a.astype(F32).reshape(bd, -1, 1)
    out = pl.pallas_call(
        functools.partial(_fox_dec_kernel, gp=gp),
        grid_spec=pltpu.PrefetchScalarGridSpec(
            num_scalar_prefetch=1,
            grid=(bd, nj),
            in_specs=in_specs,
            out_specs=col(FOX_W),
            scratch_shapes=[pltpu.VMEM((FOX_H, 1), F32), pltpu.VMEM((FOX_H, 1), F32),
                            pltpu.VMEM((FOX_W, PAGE), F32), pltpu.VMEM((FOX_H, 1), F32)]),
        out_shape=jax.ShapeDtypeStruct((bd, FOX_W, 1), F32),
        compiler_params=_cparams(("parallel", "arbitrary")),
        name="fox_decode",
    )(page_table, qbd, c3(logf8), kn.reshape(bd, 1, FOX_W), c3(v), c3(gate), u_mat,
      *([ck] * gp), *([cv] * gp), *([clf] * gp))
    return out.reshape(bd, FOX_W)


def _mla_dec_kernel(pt_ref, qa_ref, qr_ref, cnew_ref, pnew_ref, wkt_ref, *rest, gp):
    c_refs = rest[0:gp]
    p_refs = rest[gp:2 * gp]
    o_ref, m_ref, l_ref, acc_ref = rest[2 * gp:]
    j = pl.program_id(1)
    nj = pl.num_programs(1)

    @pl.when(j == 0)
    def _():
        m_ref[...] = jnp.full_like(m_ref, NEG_INF)
        l_ref[...] = jnp.zeros_like(l_ref)
        acc_ref[...] = jnp.zeros_like(acc_ref)

    qa = qa_ref[0]
    qr = qr_ref[0]

    def scores_t(cb, pt):
        n = cb.shape[0]
        kn = _dot_nt(wkt_ref[...], cb)
        ssq = jnp.sum((kn * kn).reshape(MLA_H, MLA_NOPE, n), axis=1)
        ssq = ssq + jnp.sum(pt * pt, axis=0, keepdims=True)
        raw = _dot_nt(qa, cb) + _dot(qr, pt.astype(BF16))
        return raw * lax.rsqrt(ssq * (1.0 / MLA_QK) + RMS_EPS)

    scores = []
    cbs = []
    for g in range(0, gp, 2):
        cb = jnp.concatenate([c_refs[g][0].astype(BF16), c_refs[g + 1][0].astype(BF16)], axis=0)
        pt = jnp.concatenate([p_refs[g][0], p_refs[g + 1][0]], axis=1)
        scores.append(scores_t(cb, pt))
        cbs.append(cb)
    m_prev = m_ref[...]
    m_new = m_prev
    for s in scores:
        m_new = jnp.maximum(m_new, jnp.max(s, axis=1, keepdims=True))
    alpha = jnp.exp(m_prev - m_new)
    l_new = alpha * l_ref[...]
    acc = alpha * acc_ref[...]
    for s, cb in zip(scores, cbs):
        p = jnp.exp(s - m_new)
        l_new = l_new + jnp.sum(p, axis=1, keepdims=True)
        acc = acc + _dot(p.astype(BF16), cb)
    m_ref[...] = m_new
    l_ref[...] = l_new
    acc_ref[...] = acc

    @pl.when(j == nj - 1)
    def _():
        cnew = cnew_ref[0]
        cblk = jnp.broadcast_to(cnew, (PAGE, MLA_KV_LORA)).astype(BF16)
        pblk = jnp.broadcast_to(pnew_ref[0], (MLA_ROPE, PAGE))
        s_new = scores_t(cblk, pblk)[:, 0:1]
        m_fin = jnp.maximum(m_new, s_new)
        a2 = jnp.exp(m_new - m_fin)
        p_new = jnp.exp(s_new - m_fin)
        l_fin = a2 * l_new + p_new
        num = a2 * acc + p_new * cnew
        o_ref[0] = num / l_fin


def _mla_decode(qa, qr, c_new, kpe_new, wkt_bf16, cache_c, cache_p, layer, page_table, *, gp=64):
    bd = qa.shape[0]
    n_pages = page_table.shape[1]
    n_pool = cache_c.shape[1]
    gp = min(gp, n_pages)
    nj = n_pages // gp
    cc = cache_c.reshape(-1, PAGE, MLA_KV_LORA)
    cp = jnp.transpose(cache_p, (0, 1, 3, 2)).reshape(-1, MLA_ROPE, PAGE)
    base = layer * n_pool
    full = lambda *s: pl.BlockSpec(s, lambda b, j, pt: (0,) * len(s))
    per_b = lambda *s: pl.BlockSpec((1,) + s, lambda b, j, pt: (b, 0, 0))

    def page_spec(r, c, g):
        return pl.BlockSpec((1, r, c), lambda b, j, pt: (base + pt[b, j * gp + g], 0, 0))

    in_specs = ([per_b(MLA_H, MLA_KV_LORA), per_b(MLA_H, MLA_ROPE), per_b(1, MLA_KV_LORA), per_b(MLA_ROPE, 1),
                 full(MLA_H * MLA_NOPE, MLA_KV_LORA)]
                + [page_spec(PAGE, MLA_KV_LORA, g) for g in range(gp)]
                + [page_spec(MLA_ROPE, PAGE, g) for g in range(gp)])
    lat = pl.pallas_call(
        functools.partial(_mla_dec_kernel, gp=gp),
        grid_spec=pltpu.PrefetchScalarGridSpec(
            num_scalar_prefetch=1,
            grid=(bd, nj),
            in_specs=in_specs,
            out_specs=per_b(MLA_H, MLA_KV_LORA),
            scratch_shapes=[pltpu.VMEM((MLA_H, 1), F32), pltpu.VMEM((MLA_H, 1), F32),
                            pltpu.VMEM((MLA_H, MLA_KV_LORA), F32)]),
        out_shape=jax.ShapeDtypeStruct((bd, MLA_H, MLA_KV_LORA), F32),
        compiler_params=_cparams(("parallel", "arbitrary")),
        name="mla_decode",
    )(page_table, qa, qr, c_new.reshape(bd, 1, MLA_KV_LORA), kpe_new.reshape(bd, MLA_ROPE, 1),
      wkt_bf16, *([cc] * gp), *([cp] * gp))
    return lat


def _latent_out_kernel(lat_ref, wv_ref, g_ref, o_ref):
    for h in range(MLA_H):
        o = _dot(lat_ref[:, h * MLA_KV_LORA:(h + 1) * MLA_KV_LORA].astype(BF16), wv_ref[h])
        sl = slice(h * MLA_V, (h + 1) * MLA_V)
        o_ref[:, sl] = o * _silu(g_ref[:, sl])


def _latent_out(lat, wv_bf16, gate):
    bd = lat.shape[0]
    return pl.pallas_call(
        _latent_out_kernel,
        out_shape=jax.ShapeDtypeStruct((bd, MLA_W), F32),
        compiler_params=pltpu.CompilerParams(vmem_limit_bytes=VMEM_LIMIT),
        name="latent_out",
    )(lat.reshape(bd, MLA_H * MLA_KV_LORA), wv_bf16, gate)


def _out_kernel(*refs, widths):
    n = len(widths)
    part_refs = refs[:n]
    w_ref, x_ref, o_ref = refs[n:]
    acc = x_ref[...]
    off = 0
    for p_ref, wd in zip(part_refs, widths):
        acc = acc + _dot(p_ref[...].astype(BF16), w_ref[off:off + wd, :])
        off += wd
    o_ref[...] = acc


def _out_proj(parts, w_bf16, x, *, tm=512):
    m, d = x.shape
    tm = min(tm, m)
    widths = tuple(p.shape[1] for p in parts)
    row = lambda w: pl.BlockSpec((tm, w), lambda i: (i, 0))
    return pl.pallas_call(
        functools.partial(_out_kernel, widths=widths),
        grid=(m // tm,),
        in_specs=[row(w) for w in widths] + [pl.BlockSpec(w_bf16.shape, lambda i: (0, 0)), row(d)],
        out_specs=row(d),
        out_shape=jax.ShapeDtypeStruct((m, d), F32),
        compiler_params=_cparams(("parallel",)),
        name="out_proj",
    )(*parts, w_bf16, x)


def _even_weights(w_in, b_f, q_gain, k_gain, w_g2):
    o = np.cumsum((0, FOX_W, FOX_W, FOX_W, FOX_H, FOX_W, GLA_KW, GLA_KW, GLA_W, GLA_RANK, GLA_W))
    seg = lambda i: w_in[:, o[i]:o[i + 1]]
    pad = jnp.zeros((w_in.shape[0], 128 - FOX_H - GLA_RANK), w_in.dtype)
    w = jnp.concatenate([seg(0), seg(1), seg(2), seg(4), seg(5), seg(6), seg(7), seg(9), seg(3), seg(8), pad],
                        axis=1).astype(BF16)
    bias128 = jnp.zeros((1, 128), F32).at[0, :FOX_H].set(b_f)
    qg = jnp.tile(q_gain, FOX_H).reshape(1, FOX_W)
    kg = jnp.tile(k_gain, FOX_H).reshape(1, FOX_W)
    w2pad = jnp.zeros((128, GLA_KW), F32).at[FOX_H:FOX_H + GLA_RANK].set(w_g2).astype(BF16)
    return w, bias128, qg, kg, w2pad


def _even_layer(xp, xs, e, cache_k, cache_v, cache_lf, state, page_table, norm, w_in, b_f, q_gain, k_gain,
                w_g2, b_g, out_gain, w_out):
    bp, tp, d = xp.shape
    bd = xs.shape[0]
    w, bias128, qg, kg, w2pad = _even_weights(w_in, b_f, q_gain, k_gain, w_g2)
    bg = b_g.reshape(1, GLA_KW)
    gain = out_gain.reshape(1, GLA_DV)
    w_out_bf = w_out.astype(BF16)

    x2 = xp.reshape(bp * tp, d)
    qa, ka, kn, v, vt, rest, misc = _even_in(x2, norm, w, bias128, qg, kg, tp, FOX_SCALE * LOG2E)
    r3 = lambda a: a.reshape(bp, tp, a.shape[-1])
    rest3 = r3(rest)
    fox = _flash(r3(qa), r3(ka), vt, rest3, 0, FOX_H, FOX_D, "flash_fox")
    gla, sfin = _gla_prompt(rest3, r3(misc), w2pad, bg, gain)
    xp_new = _out_proj([fox.reshape(bp * tp, FOX_W), gla.reshape(bp * tp, GLA_W)], w_out_bf, x2).reshape(bp, tp, d)
    tok_major = lambda a: a.reshape(bp, FOX_H, FOX_D, tp).transpose(0, 3, 1, 2)
    outs_p = (tok_major(kn), tok_major(v),
              r3(misc)[:, :, :FOX_H], sfin.transpose(0, 1, 3, 2))

    xs2 = xs.reshape(bd, d)
    qa, ka, kn, v, vt, rest, misc = _even_in(xs2, norm, w, bias128, qg, kg, 1, FOX_SCALE)
    qn = qa.reshape(bd, FOX_H, HP)[:, :, :FOX_D].reshape(bd, FOX_W)
    fox = _fox_decode(qn, kn, v, misc[:, :FOX_H], rest[:, :FOX_W], cache_k, cache_v, cache_lf, e, page_table)
    gla, snew = _gla_step(rest, misc, state, w_g2, b_g, gain)
    xs_new = _out_proj([fox, gla], w_out_bf, xs2).reshape(bd, 1, d)
    outs_s = (kn.reshape(bd, 1, FOX_H, FOX_D), v.reshape(bd, 1, FOX_H, FOX_D),
              misc[:, :FOX_H].reshape(bd, 1, FOX_H), snew)
    return xp_new, xs_new, outs_p, outs_s


def _odd_weights(w_in, w_q_b, w_kv_b, q_gain, k_gain):
    rows = w_in.shape[0]
    o = np.cumsum((0, MLA_Q_LORA, MLA_KV_LORA, MLA_ROPE, MLA_W))
    z = lambda n: jnp.zeros((rows, n), w_in.dtype)
    w = jnp.concatenate([w_in[:, o[0]:o[1]], w_in[:, o[1]:o[2]], z(ROPE_LO), w_in[:, o[2]:o[3]],
                         z(HP - ROPE_LO - MLA_ROPE), w_in[:, o[3]:o[4]]], axis=1).astype(BF16)
    wq = jnp.pad(w_q_b.reshape(MLA_Q_LORA, MLA_H, MLA_QK), ((0, 0), (0, 0), (0, HP - MLA_QK)))
    wq = wq.reshape(MLA_Q_LORA, MLA_H * HP).astype(BF16)
    wkv = w_kv_b.reshape(MLA_KV_LORA, MLA_H, MLA_NOPE + MLA_V)
    wk = wkv[:, :, :MLA_NOPE]
    wk_pad = jnp.pad(wk, ((0, 0), (0, 0), (0, HP - MLA_NOPE))).reshape(MLA_KV_LORA, MLA_H * HP).astype(BF16)
    wv = wkv[:, :, MLA_NOPE:]
    padg = lambda g: jnp.pad(g, (0, HP - MLA_QK)).reshape(1, HP)
    return w, wq, wk, wk_pad, wv, padg(q_gain), padg(k_gain)


def _rope_tiles(pos):
    inv = ROPE_THETA ** (-jnp.arange(ROPE_HALF, dtype=F32) / ROPE_HALF)
    ang = pos.astype(F32)[:, None] * inv[None, :]
    cos, sin = jnp.cos(ang), jnp.sin(ang)
    n = pos.shape[0]
    one = jnp.ones((n, ROPE_LO), F32)
    z = lambda w: jnp.zeros((n, w), F32)
    tail = HP - ROPE_LO - MLA_ROPE
    cos_t = jnp.concatenate([one, cos, cos, z(tail)], axis=1)
    sa_t = jnp.concatenate([z(ROPE_LO + ROPE_HALF), sin, z(tail)], axis=1)
    sb_t = jnp.concatenate([z(ROPE_LO), -sin, z(ROPE_HALF + tail)], axis=1)
    return cos_t, sa_t, sb_t


def _odd_layer(xp, xs, o, cache_c, cache_p, page_table, norm, w_in, q_a_gain, w_q_b, kv_a_gain, w_kv_b,
               q_gain, k_gain, w_out):
    bp, tp, d = xp.shape
    bd = xs.shape[0]
    past = page_table.shape[1] * PAGE
    w, wq, wk, wk_pad, wv, qg, kg = _odd_weights(w_in, w_q_b, w_kv_b, q_gain, k_gain)
    wv_flat = wv.reshape(MLA_KV_LORA, MLA_W).astype(BF16)
    w_out_bf = w_out.astype(BF16)

    x2 = xp.reshape(bp * tp, d)
    q, k, vt, c, kpe, gate = _odd_in(x2, norm, w, q_a_gain, wq, kv_a_gain, wk_pad, wv_flat, qg, kg,
                                     _rope_tiles(jnp.arange(tp)), MLA_SCALE * LOG2E)
    r3 = lambda a: a.reshape(bp, tp, a.shape[-1])
    att = _flash(r3(q), r3(k), vt, r3(gate), 0, MLA_H, MLA_V, "flash_mla")
    xp_new = _out_proj([att.reshape(bp * tp, MLA_W)], w_out_bf, x2).reshape(bp, tp, d)
    outs_p = (c.reshape(bp, tp, MLA_KV_LORA), kpe.reshape(bp, tp, MLA_ROPE))

    xs2 = xs.reshape(bd, d)
    q, _, _, c, kpe, gate = _odd_in(xs2, norm, w, q_a_gain, wq, kv_a_gain, wk_pad, wv_flat, qg, kg,
                                    _rope_tiles(jnp.full((bd,), past, jnp.int32)), MLA_SCALE)
    qk = q.astype(F32).reshape(bd, MLA_H, HP)[:, :, :MLA_QK] * k_gain
    qa = jnp.einsum('bhd,rhd->bhr', qk[:, :, :MLA_NOPE], wk).astype(BF16)
    qr = qk[:, :, MLA_NOPE:].astype(BF16)
    wkt = wk.reshape(MLA_KV_LORA, MLA_H * MLA_NOPE).T.astype(BF16)
    lat = _mla_decode(qa, qr, c, kpe, wkt, cache_c, cache_p, o, page_table)
    att = _latent_out(lat, wv.transpose(1, 0, 2).astype(BF16), gate)
    xs_new = _out_proj([att], w_out_bf, xs2).reshape(bd, 1, d)
    outs_s = (c.reshape(bd, 1, MLA_KV_LORA), kpe.reshape(bd, 1, MLA_ROPE))
    return xp_new, xs_new, outs_p, outs_s


def kernel(x_prompt, x_sample, cache_fox_k, cache_fox_v, cache_fox_logf, cache_mla_ckv, cache_mla_kpe,
           state_gla, page_table, norm_even, w_in_even, b_fox_f, fox_q_gain, fox_k_gain, gla_w_gate2,
           gla_b_gate, gla_out_gain, w_out_even, norm_odd, w_in_odd, mla_q_a_gain, w_q_b, mla_kv_a_gain,
           w_kv_b, mla_q_gain, mla_k_gain, w_out_odd):
    depth = norm_even.shape[0] + norm_odd.shape[0]
    xp, xs = x_prompt, x_sample
    ev_p, ev_s, od_p, od_s = [], [], [], []
    for layer in range(depth):
        i = layer // 2
        if layer % 2 == 0:
            xp, xs, op, os_ = _even_layer(
                xp, xs, i, cache_fox_k, cache_fox_v, cache_fox_logf, state_gla[i], page_table,
                norm_even[i], w_in_even[i], b_fox_f[i], fox_q_gain[i], fox_k_gain[i], gla_w_gate2[i],
                gla_b_gate[i], gla_out_gain[i], w_out_even[i])
            ev_p.append(op)
            ev_s.append(os_)
        else:
            xp, xs, op, os_ = _odd_layer(
                xp, xs, i, cache_mla_ckv, cache_mla_kpe, page_table, norm_odd[i], w_in_odd[i],
                mla_q_a_gain[i], w_q_b[i], mla_kv_a_gain[i], w_kv_b[i], mla_q_gain[i], mla_k_gain[i],
                w_out_odd[i])
            od_p.append(op)
            od_s.append(os_)
    st = lambda lst, k: jnp.stack([t[k] for t in lst])
    return (xp, xs,
            st(ev_p, 0), st(ev_p, 1), st(ev_p, 2), st(od_p, 0), st(od_p, 1), st(ev_p, 3),
            st(ev_s, 0), st(ev_s, 1), st(ev_s, 2), st(od_s, 0), st(od_s, 1), st(ev_s, 3))
```
